```python
import jax, jax.numpy as jnp
from jax import lax
import numpy as np

D_MODEL = 1024
BATCH = 1
SEQ = 16384
DEPTH = 4

N_MIXERS = 3
SWA_HEADS = 16
SWA_KV_HEADS = 4
SWA_HEAD_DIM = 64
SWA_WINDOW = 128
SWA_QKV_DIM = (SWA_HEADS + 2 * SWA_KV_HEADS) * SWA_HEAD_DIM
HGRN_HEADS = 8
HGRN_EXPAND = 128
HGRN_HEAD_DIM = D_MODEL // HGRN_HEADS
HGRN_CHUNK = 64
HGRN_IN_DIM = 2 * HGRN_HEADS * HGRN_EXPAND + 2 * HGRN_HEADS * HGRN_HEAD_DIM
FOX_HEADS = 16
FOX_HEAD_DIM = 64
FOX_BLOCK = 128
FOX_IN_DIM = 3 * FOX_HEADS * FOX_HEAD_DIM + FOX_HEADS
FOX_FORGET_BIAS_INIT = 2.0
D_FF = 4 * D_MODEL
EPS = 1e-6

N_A = (DEPTH + 2) // 3
N_B = (DEPTH + 1) // 3
N_C = DEPTH // 3

kernel_name = "hybrid_swa_hgrn2_fox_trunk"


def rms_norm(x, g):
    xf = x.astype(jnp.float32)
    y = xf * lax.rsqrt(jnp.mean(xf * xf, axis=-1, keepdims=True) + EPS)
    return (y * g.astype(jnp.float32)).astype(x.dtype)


def swa_sink_attention(h, w_qkv, b_qkv, sinks, w_o):
    B, T, _ = h.shape
    W, KV, dh = SWA_WINDOW, SWA_KV_HEADS, SWA_HEAD_DIM
    G = SWA_HEADS // KV
    nb = T // W
    qkv = h @ w_qkv + b_qkv
    q, k, v = jnp.split(qkv, [SWA_HEADS * dh, (SWA_HEADS + KV) * dh], axis=-1)
    q = q.reshape(B, nb, W, KV, G, dh) * (dh ** -0.5)
    k = k.reshape(B, nb, W, KV, dh)
    v = v.reshape(B, nb, W, KV, dh)
    k_band = jnp.concatenate([jnp.pad(k[:, :-1], ((0, 0), (1, 0), (0, 0), (0, 0), (0, 0))), k], axis=2)
    v_band = jnp.concatenate([jnp.pad(v[:, :-1], ((0, 0), (1, 0), (0, 0), (0, 0), (0, 0))), v], axis=2)
    s = jnp.einsum('bnqhgd,bnkhd->bnhgqk', q, k_band).astype(jnp.float32)
    qi = jnp.arange(W)[:, None]
    kk = jnp.arange(2 * W)[None, :]
    rel = qi + W - kk
    in_window = (rel >= 0) & (rel < W)
    blk = jnp.arange(nb)[:, None, None]
    mask = in_window[None] & ((blk > 0) | (kk[None] >= W))
    s = jnp.where(mask[None, :, None, None], s, -jnp.inf)
    sink = sinks.astype(jnp.float32).reshape(KV, G)[None, None, :, :, None, None]
    sink = jnp.broadcast_to(sink, s.shape[:-1] + (1,))
    p = jax.nn.softmax(jnp.concatenate([s, sink], axis=-1), axis=-1)[..., :-1]
    o = jnp.einsum('bnhgqk,bnkhd->bnqhgd', p.astype(v.dtype), v_band)
    return o.reshape(B, T, SWA_HEADS * dh) @ w_o


def hgrn2_mixer(h, w_in, lb, g_norm, w_o):
    B, T, _ = h.shape
    H, K, V, C = HGRN_HEADS, HGRN_EXPAND, HGRN_HEAD_DIM, HGRN_CHUNK
    nc = T // C
    proj = h @ w_in
    q, f_logit, i_in, g = jnp.split(proj, [H * K, 2 * H * K, 2 * H * K + H * V], axis=-1)
    f = lb + (1.0 - lb) * jax.nn.sigmoid(f_logit.astype(jnp.float32))
    log_f = jnp.log(f)
    k = 1.0 - f

    def to_chunks(a, d):
        return a.astype(jnp.float32).reshape(B, nc, C, H, d).transpose(1, 0, 3, 2, 4)

    xs = (to_chunks(jax.nn.silu(q), K), to_chunks(k, K), to_chunks(i_in, V), to_chunks(log_f, K))
    causal = jnp.tril(jnp.ones((C, C), dtype=bool))[:, :, None]

    def step(S, inp):
        qc, kc, vc, gc = inp
        bcum = jnp.cumsum(gc, axis=2)
        diff = jnp.where(causal, bcum[:, :, :, None, :] - bcum[:, :, None, :, :], -jnp.inf)
        A = jnp.einsum('bhtk,bhsk,bhtsk->bhts', qc, kc, jnp.exp(diff))
        o = jnp.einsum('bhts,bhsv->bhtv', A, vc) + jnp.einsum('bhtk,bhkv->bhtv', qc * jnp.exp(bcum), S)
        b_last = bcum[:, :, -1]
        S_new = jnp.exp(b_last)[..., None] * S + jnp.einsum(
            'bhsk,bhsv->bhkv', kc * jnp.exp(b_last[:, :, None, :] - bcum), vc)
        return S_new, o

    S0 = jnp.zeros((B, H, K, V), jnp.float32)
    _, o = lax.scan(step, S0, xs)
    o = o.transpose(1, 0, 3, 2, 4).reshape(B, T, H, V)
    o = o * lax.rsqrt(jnp.mean(o * o, axis=-1, keepdims=True) + EPS)
    o = o.reshape(B, T, H * V) * g_norm.astype(jnp.float32) * jax.nn.silu(g.astype(jnp.float32))
    return o.astype(h.dtype) @ w_o


def fox_attention(h, w_in, b_in, w_o):
    B, T, _ = h.shape
    H, dh, Q = FOX_HEADS, FOX_HEAD_DIM, FOX_BLOCK
    nb = T // Q
    proj = h @ w_in + b_in
    q, k, v, f_logit = jnp.split(proj, [H * dh, 2 * H * dh, 3 * H * dh], axis=-1)
    log_f = jax.nn.log_sigmoid(f_logit.astype(jnp.float32))
    c = jnp.cumsum(log_f, axis=1).transpose(0, 2, 1)
    q = q.reshape(B, T, H, dh) * (dh ** -0.5)
    k = k.reshape(B, T, H, dh)
    v = v.reshape(B, T, H, dh)
    q_blocks = q.reshape(B, nb, Q, H, dh).transpose(1, 0, 2, 3, 4)
    c_blocks = c.reshape(B, H, nb, Q).transpose(2, 0, 1, 3)
    pos_blocks = jnp.arange(T).reshape(nb, Q)
    kpos = jnp.arange(T)

    def block(args):
        qb, cb, pb = args
        s = jnp.einsum('bqhd,bkhd->bhqk', qb, k).astype(jnp.float32)
        s = s + (cb[..., :, None] - c[..., None, :])
        s = jnp.where((pb[:, None] >= kpos[None, :])[None, None], s, -jnp.inf)
        p = jax.nn.softmax(s, axis=-1)
        return jnp.einsum('bhqk,bkhd->bqhd', p.astype(v.dtype), v)

    o = lax.map(block, (q_blocks, c_blocks, pos_blocks))
    o = o.transpose(1, 0, 2, 3, 4).reshape(B, T, H * dh)
    return o @ w_o


def setup_inputs(seed: int = 0) -> dict:
    key = jax.random.key(seed)
    ks = jax.random.split(key, 18)
    f32 = jnp.float32

    def nrm(k, shape, fan_in):
        return jax.random.normal(k, shape, f32) * (fan_in ** -0.5)

    def gain(k, shape):
        return 1.0 + 0.02 * jax.random.normal(k, shape, f32)

    fox_b_in = 0.02 * jax.random.normal(ks[16], (N_C, FOX_IN_DIM), f32)
    fox_b_in = fox_b_in.at[:, 3 * FOX_HEADS * FOX_HEAD_DIM:].add(FOX_FORGET_BIAS_INIT)
    return {
        "x": jax.random.normal(ks[0], (BATCH, SEQ, D_MODEL), f32),
        "norm_mix": gain(ks[1], (DEPTH, D_MODEL)),
        "norm_mlp": gain(ks[2], (DEPTH, D_MODEL)),
        "norm_final": gain(ks[3], (D_MODEL,)),
        "w_up": nrm(ks[4], (DEPTH, D_MODEL, D_FF), D_MODEL),
        "w_down": nrm(ks[5], (DEPTH, D_FF, D_MODEL), D_FF),
        "swa_w_qkv": nrm(ks[6], (N_A, D_MODEL, SWA_QKV_DIM), D_MODEL),
        "swa_b_qkv": 0.02 * jax.random.normal(ks[7], (N_A, SWA_QKV_DIM), f32),
        "swa_sinks": 0.5 * jax.random.normal(ks[8], (N_A, SWA_HEADS), f32),
        "swa_w_o": nrm(ks[9], (N_A, SWA_HEADS * SWA_HEAD_DIM, D_MODEL), SWA_HEADS * SWA_HEAD_DIM),
        "hgrn_w_in": nrm(ks[10], (N_B, D_MODEL, HGRN_IN_DIM), D_MODEL),
        "hgrn_lb_logits": 0.1 * jax.random.normal(ks[11], (DEPTH, HGRN_HEADS * HGRN_EXPAND), f32),
        "hgrn_g_norm": gain(ks[12], (N_B, HGRN_HEADS * HGRN_HEAD_DIM)),
        "hgrn_w_o": nrm(ks[13], (N_B, HGRN_HEADS * HGRN_HEAD_DIM, D_MODEL), HGRN_HEADS * HGRN_HEAD_DIM),
        "fox_w_in": nrm(ks[14], (N_C, D_MODEL, FOX_IN_DIM), D_MODEL),
        "fox_b_in": fox_b_in,
        "fox_w_o": nrm(ks[15], (N_C, FOX_HEADS * FOX_HEAD_DIM, D_MODEL), FOX_HEADS * FOX_HEAD_DIM),
    }


def reference(x, norm_mix, norm_mlp, norm_final, w_up, w_down,
              swa_w_qkv, swa_b_qkv, swa_sinks, swa_w_o,
              hgrn_w_in, hgrn_lb_logits, hgrn_g_norm, hgrn_w_o,
              fox_w_in, fox_b_in, fox_w_o):
    lb_soft = jax.nn.softmax(hgrn_lb_logits.astype(jnp.float32), axis=0)
    lower_bounds = jnp.cumsum(lb_soft, axis=0) - lb_soft[0]
    for i in range(DEPTH):
        h = rms_norm(x, norm_mix[i])
        m, j = i % N_MIXERS, i // N_MIXERS
        if m == 0:
            y = swa_sink_attention(h, swa_w_qkv[j], swa_b_qkv[j], swa_sinks[j], swa_w_o[j])
        elif m == 1:
            y = hgrn2_mixer(h, hgrn_w_in[j], lower_bounds[i], hgrn_g_norm[j], hgrn_w_o[j])
        else:
            y = fox_attention(h, fox_w_in[j], fox_b_in[j], fox_w_o[j])
        x = x + y
        h = rms_norm(x, norm_mlp[i])
        x = x + jnp.square(jax.nn.relu(h @ w_up[i])) @ w_down[i]
    return rms_norm(x, norm_final)
```

```python
import functools

import jax
import jax.numpy as jnp
from jax import lax
from jax.experimental import pallas as pl
from jax.experimental.pallas import tpu as pltpu

F32 = jnp.float32
BF16 = jnp.bfloat16

D_MODEL = 1024
D_FF = 4 * D_MODEL
EPS = 1e-6
N_MIXERS = 3

SWA_HEADS = 16
SWA_KV_HEADS = 4
SWA_HEAD_DIM = 64
SWA_WINDOW = 128

HGRN_HEADS = 8
HGRN_EXPAND = 128
HGRN_HEAD_DIM = 128
HGRN_CHUNK = 64
HGRN_SUB = 16

FOX_HEADS = 16
FOX_HEAD_DIM = 64

LANES = 128
VMEM_LIMIT = 56 * 1024 * 1024
NEG_BIG = -1e30

ROW_TILE = 512


def _cparams(sem):
    return pltpu.CompilerParams(dimension_semantics=sem, vmem_limit_bytes=VMEM_LIMIT)


def _const_spec(shape):
    nd = len(shape)
    return pl.BlockSpec(shape, lambda *_: (0,) * nd, pipeline_mode=pl.Buffered(1))


def _rms(x, g):
    ms = jnp.mean(x * x, axis=-1, keepdims=True)
    return x * lax.rsqrt(ms + EPS) * g


def _norm_proj_kernel(x_ref, g_ref, w_ref, b_ref, *o_refs, segments, n_chunk):
    h = _rms(x_ref[...], g_ref[...]).astype(BF16)
    for o_ref, (lo, hi) in zip(o_refs, segments):
        for j in range(lo, hi, n_chunk):
            je = min(j + n_chunk, hi)
            y = jnp.dot(h, w_ref[:, j:je], preferred_element_type=F32) + b_ref[:, j:je]
            o_ref[:, j - lo:je - lo] = y.astype(o_ref.dtype)


def norm_proj(x, g, w, b, segments, dtypes, name):
    T, D = x.shape
    N = w.shape[1]
    tm = ROW_TILE
    out_shape = [jax.ShapeDtypeStruct((T, hi - lo), dt) for (lo, hi), dt in zip(segments, dtypes)]
    out_specs = [pl.BlockSpec((tm, hi - lo), lambda i: (i, 0)) for (lo, hi) in segments]
    return pl.pallas_call(
        functools.partial(_norm_proj_kernel, segments=tuple(segments), n_chunk=512),
        grid=(T // tm,),
        in_specs=[
            pl.BlockSpec((tm, D), lambda i: (i, 0)),
            _const_spec((1, D)),
            _const_spec((D, N)),
            _const_spec((1, N)),
        ],
        out_specs=out_specs,
        out_shape=out_shape,
        compiler_params=_cparams(("parallel",)),
        name=name,
    )(x, g.reshape(1, D), w, b.reshape(1, N))


def _proj_residual_kernel(a_ref, w_ref, x_ref, o_ref):
    a = a_ref[...].astype(BF16)
    o_ref[...] = x_ref[...] + jnp.dot(a, w_ref[...], preferred_element_type=F32)


def proj_residual(a, w, x, name):
    T, K = a.shape
    D = w.shape[1]
    tm = ROW_TILE
    return pl.pallas_call(
        _proj_residual_kernel,
        grid=(T // tm,),
        in_specs=[
            pl.BlockSpec((tm, K), lambda i: (i, 0)),
            _const_spec((K, D)),
            pl.BlockSpec((tm, D), lambda i: (i, 0)),
        ],
        out_specs=pl.BlockSpec((tm, D), lambda i: (i, 0)),
        out_shape=jax.ShapeDtypeStruct((T, D), F32),
        compiler_params=_cparams(("parallel",)),
        name=name,
    )(a, w, x)


def _hgrn_out_kernel(o_ref, g_ref, gn_ref, w_ref, x_ref, y_ref):
    parts = []
    for h in range(HGRN_HEADS):
        sl = slice(h * HGRN_HEAD_DIM, (h + 1) * HGRN_HEAD_DIM)
        o = o_ref[:, sl]
        o = o * lax.rsqrt(jnp.mean(o * o, axis=-1, keepdims=True) + EPS)
        g = g_ref[:, sl]
        parts.append((o * gn_ref[:, sl] * (g * jax.nn.sigmoid(g))).astype(BF16))
    a = jnp.concatenate(parts, axis=1)
    y_ref[...] = x_ref[...] + jnp.dot(a, w_ref[...], preferred_element_type=F32)


def hgrn_out(o, g, g_norm, w, x):
    T, D = x.shape
    tm = ROW_TILE
    row = lambda i: (i, 0)
    return pl.pallas_call(
        _hgrn_out_kernel,
        grid=(T // tm,),
        in_specs=[
            pl.BlockSpec((tm, D), row),
            pl.BlockSpec((tm, D), row),
            _const_spec((1, D)),
            _const_spec((D, D)),
            pl.BlockSpec((tm, D), row),
        ],
        out_specs=pl.BlockSpec((tm, D), row),
        out_shape=jax.ShapeDtypeStruct((T, D), F32),
        compiler_params=_cparams(("parallel",)),
        name="hgrn_out",
    )(o, g, g_norm.reshape(1, D), w, x)


def _mlp_kernel(x_ref, g_ref, wu_ref, wd_ref, gf_ref, o_ref, *, ff_chunk, final_norm):
    x = x_ref[...]
    h = _rms(x, g_ref[...]).astype(BF16)
    acc = x
    for j in range(0, D_FF, ff_chunk):
        u = jnp.dot(h, wu_ref[:, j:j + ff_chunk], preferred_element_type=F32)
        u = jnp.maximum(u, 0.0)
        u = (u * u).astype(BF16)
        acc = acc + jnp.dot(u, wd_ref[j:j + ff_chunk, :], preferred_element_type=F32)
    if final_norm:
        acc = _rms(acc, gf_ref[...])
    o_ref[...] = acc


def mlp(x, g, w_up, w_down, g_final, final_norm):
    T, D = x.shape
    tm = ROW_TILE
    return pl.pallas_call(
        functools.partial(_mlp_kernel, ff_chunk=512, final_norm=final_norm),
        grid=(T // tm,),
        in_specs=[
            pl.BlockSpec((tm, D), lambda i: (i, 0)),
            _const_spec((1, D)),
            _const_spec((D, D_FF)),
            _const_spec((D_FF, D)),
            _const_spec((1, D)),
        ],
        out_specs=pl.BlockSpec((tm, D), lambda i: (i, 0)),
        out_shape=jax.ShapeDtypeStruct((T, D), F32),
        compiler_params=_cparams(("parallel",)),
        name="mlp",
    )(x, g.reshape(1, D), w_up, w_down, g_final.reshape(1, D))


def _swa_kernel(sink_ref, q_ref, kc_ref, kp_ref, vc_ref, vp_ref, o_ref):
    n = pl.program_id(0)
    W, dh = SWA_WINDOW, SWA_HEAD_DIM
    G = SWA_HEADS // SWA_KV_HEADS
    qi = lax.broadcasted_iota(jnp.int32, (W, 2 * W), 0)
    kk = lax.broadcasted_iota(jnp.int32, (W, 2 * W), 1)
    rel = qi + W - kk
    visible = (rel >= 0) & (rel < W) & ((n > 0) | (kk >= W))
    outs = []
    for hk in range(SWA_KV_HEADS):
        ks = slice(hk * dh, (hk + 1) * dh)
        kb = jnp.concatenate([kp_ref[:, ks], kc_ref[:, ks]], axis=0)
        vb = jnp.concatenate([vp_ref[:, ks], vc_ref[:, ks]], axis=0)
        for g in range(G):
            head = hk * G + g
            q = q_ref[:, head * dh:(head + 1) * dh] * (dh ** -0.5)
            s = lax.dot_general(q, kb, (((1,), (1,)), ((), ())), preferred_element_type=F32)
            s = jnp.where(visible, s, NEG_BIG)
            sink = sink_ref[head]
            m = jnp.maximum(jnp.max(s, axis=-1, keepdims=True), sink)
            p = jnp.exp(s - m)
            denom = jnp.sum(p, axis=-1, keepdims=True) + jnp.exp(sink - m)
            o = jnp.dot(p.astype(BF16), vb, preferred_element_type=F32)
            outs.append(o / denom)
    o_ref[...] = jnp.concatenate(outs, axis=1).astype(o_ref.dtype)


def swa_attention(q, k, v, sinks):
    T = q.shape[0]
    W = SWA_WINDOW
    kvw = SWA_KV_HEADS * SWA_HEAD_DIM
    cur = lambda n: (n, 0)
    prev = lambda n: (jnp.maximum(n - 1, 0), 0)
    return pl.pallas_call(
        _swa_kernel,
        grid=(T // W,),
        in_specs=[
            pl.BlockSpec(memory_space=pltpu.SMEM),
            pl.BlockSpec((W, SWA_HEADS * SWA_HEAD_DIM), cur),
            pl.BlockSpec((W, kvw), cur),
            pl.BlockSpec((W, kvw), prev),
            pl.BlockSpec((W, kvw), cur),
            pl.BlockSpec((W, kvw), prev),
        ],
        out_specs=pl.BlockSpec((W, SWA_HEADS * SWA_HEAD_DIM), cur),
        out_shape=jax.ShapeDtypeStruct((T, SWA_HEADS * SWA_HEAD_DIM), BF16),
        compiler_params=_cparams(("parallel",)),
        name="swa",
    )(sinks, q, k, k, v, v)


def _split_dot(m_bf16, x):
    hi = x.astype(BF16)
    lo = (x - hi.astype(F32)).astype(BF16)
    return (jnp.dot(m_bf16, hi, preferred_element_type=F32)
            + jnp.dot(m_bf16, lo, preferred_element_type=F32))


def _hgrn_kernel(q_ref, f_ref, v_ref, lb_ref, o_ref, st_ref):
    C, SUB = HGRN_CHUNK, HGRN_SUB
    K = HGRN_EXPAND

    @pl.when(pl.program_id(1) == 0)
    def _():
        st_ref[...] = jnp.zeros_like(st_ref)

    lb = lb_ref[...]
    f = lb + (1.0 - lb) * jax.nn.sigmoid(f_ref[...])
    logf = jnp.log(f)
    kk = 1.0 - f
    q = q_ref[...]
    qs = q * jax.nn.sigmoid(q)
    v = v_ref[...]

    row = lax.broadcasted_iota(jnp.int32, (C, C), 0)
    col = lax.broadcasted_iota(jnp.int32, (C, C), 1)
    tril = (row >= col)
    same_sub = (row // SUB) == (col // SUB)
    bcum = _split_dot(tril.astype(BF16), logf)
    bsub = _split_dot((tril & same_sub).astype(BF16), logf)
    b_last = bcum[C - 1:C, :]

    st = st_ref[...]
    q_in = (qs * jnp.exp(bcum)).astype(BF16)
    o = lax.dot_general(q_in, st.astype(BF16), (((1,), (1,)), ((), ())), preferred_element_type=F32)
    k_out = (kk * jnp.exp(b_last - bcum)).astype(BF16)
    upd = lax.dot_general(v, k_out, (((0,), (0,)), ((), ())), preferred_element_type=F32)
    st_ref[...] = st * jnp.exp(b_last) + upd

    q_sub = (qs * jnp.exp(bsub)).astype(BF16)
    a = jnp.zeros((C, C), F32)
    for i in range(1, C // SUB):
        r_i = bcum[i * SUB - 1:i * SUB, :]
        k_i = (kk * jnp.exp(jnp.minimum(r_i - bcum, 0.0))).astype(BF16)
        a_i = lax.dot_general(q_sub, k_i, (((1,), (1,)), ((), ())), preferred_element_type=F32)
        a = jnp.where((row // SUB == i) & (col < i * SUB), a_i, a)
    o = o + jnp.dot(a.astype(BF16), v, preferred_element_type=F32)

    sub_pos = lax.broadcasted_iota(jnp.int32, (C, K), 0) % SUB
    ones = jnp.ones((K, LANES), BF16)
    vf = v.astype(F32)
    for d in range(SUB):
        if d == 0:
            e = qs * kk
            vd = vf
        else:
            e = qs * pltpu.roll(kk, d, 0) * jnp.exp(bcum - pltpu.roll(bcum, d, 0))
            e = jnp.where(sub_pos >= d, e, 0.0)
            vd = pltpu.roll(vf, d, 0)
        rs = jnp.dot(e.astype(BF16), ones, preferred_element_type=F32)
        o = o + rs * vd
    o_ref[...] = o


def hgrn_recurrence(q, f_logit, v, lb):
    T = q.shape[0]
    C = HGRN_CHUNK
    blk = lambda h, c: (c, h)
    return pl.pallas_call(
        _hgrn_kernel,
        grid=(HGRN_HEADS, T // C),
        in_specs=[
            pl.BlockSpec((C, HGRN_EXPAND), blk),
            pl.BlockSpec((C, HGRN_EXPAND), blk),
            pl.BlockSpec((C, HGRN_HEAD_DIM), blk),
            pl.BlockSpec((1, HGRN_EXPAND), lambda h, c: (0, h)),
        ],
        out_specs=pl.BlockSpec((C, HGRN_HEAD_DIM), blk),
        out_shape=jax.ShapeDtypeStruct((T, HGRN_HEADS * HGRN_HEAD_DIM), F32),
        scratch_shapes=[pltpu.VMEM((HGRN_HEAD_DIM, HGRN_EXPAND), F32)],
        compiler_params=_cparams(("parallel", "arbitrary")),
        name="hgrn",
    )(q, f_logit, v, lb.reshape(1, -1))


FOX_CUM_TILE = 256


def _fox_cumsum_kernel(f_ref, c_ref, carry_ref):
    @pl.when(pl.program_id(0) == 0)
    def _():
        carry_ref[...] = jnp.zeros_like(carry_ref)

    x = f_ref[...]
    ls = jnp.minimum(x, 0.0) - jnp.log(1.0 + jnp.exp(-jnp.abs(x)))
    n = FOX_CUM_TILE
    tril = (lax.broadcasted_iota(jnp.int32, (n, n), 0) >= lax.broadcasted_iota(jnp.int32, (n, n), 1)).astype(BF16)
    hi = ls.astype(BF16)
    r1 = ls - hi.astype(F32)
    mid = r1.astype(BF16)
    lo = (r1 - mid.astype(F32)).astype(BF16)
    cum = (jnp.dot(tril, hi, preferred_element_type=F32)
           + jnp.dot(tril, mid, preferred_element_type=F32)
           + jnp.dot(tril, lo, preferred_element_type=F32))
    c = cum + carry_ref[...]
    c_ref[...] = c
    carry_ref[...] = c[n - 1:n, :]


def fox_cumsum(f_logit):
    T = f_logit.shape[0]
    n = FOX_CUM_TILE
    return pl.pallas_call(
        _fox_cumsum_kernel,
        grid=(T // n,),
        in_specs=[pl.BlockSpec((n, LANES), lambda i: (i, 0))],
        out_specs=pl.BlockSpec((n, LANES), lambda i: (i, 0)),
        out_shape=jax.ShapeDtypeStruct((T, LANES), F32),
        scratch_shapes=[pltpu.VMEM((1, LANES), F32)],
        compiler_params=_cparams(("arbitrary",)),
        name="fox_cumsum",
    )(f_logit)


FOX_TILE = 512


def _fox_kernel(q_ref, kt_ref, v_ref, ccol_ref, crow_ref, o_ref, acc_ref, m_ref, l_ref):
    i = pl.program_id(1)
    tq = tk = FOX_TILE
    dh = FOX_HEAD_DIM
    lane = lax.broadcasted_iota(jnp.int32, (tq, LANES), 1)
    q = q_ref[...] * (dh ** -0.5)
    zero = jnp.zeros_like(q)
    q_heads = (jnp.where(lane < dh, q, zero), jnp.where(lane >= dh, q, zero))
    cc = ccol_ref[0]
    c_q = (cc[:, 0:1], cc[:, 1:2])

    acc_ref[...] = jnp.zeros_like(acc_ref)
    m_ref[...] = jnp.full_like(m_ref, NEG_BIG)
    l_ref[...] = jnp.zeros_like(l_ref)

    def block(j, masked):
        start = pl.multiple_of(j * tk, tk)
        kt = kt_ref[:, pl.ds(start, tk)]
        v = v_ref[pl.ds(start, tk), :]
        c_k = crow_ref[0, :, pl.ds(start, tk)]
        if masked:
            causal = (lax.broadcasted_iota(jnp.int32, (tq, tk), 0)
                      >= lax.broadcasted_iota(jnp.int32, (tq, tk), 1))
        for h in range(2):
            s = jnp.dot(q_heads[h], kt, preferred_element_type=F32)
            s = s + (c_q[h] - c_k[h:h + 1, :])
            if masked:
                s = jnp.where(causal, s, NEG_BIG)
            m_old = m_ref[h]
            m_new = jnp.maximum(m_old, jnp.max(s, axis=-1, keepdims=True))
            alpha = jnp.exp(m_old - m_new)
            p = jnp.exp(s - m_new)
            l_ref[h] = alpha * l_ref[h] + jnp.sum(p, axis=-1, keepdims=True)
            m_ref[h] = m_new
            pv = jnp.dot(p.astype(BF16), v, preferred_element_type=F32)
            acc_ref[h] = alpha * acc_ref[h] + pv

    def body(j, carry):
        block(j, False)
        return carry

    lax.fori_loop(0, i, body, 0)
    block(i, True)

    o_a = acc_ref[0] / l_ref[0]
    o_b = acc_ref[1] / l_ref[1]
    o_ref[...] = jnp.where(lane < dh, o_a, o_b).astype(o_ref.dtype)


def fox_attention(q, kt, v, c_col, c_row):
    T = q.shape[0]
    t = FOX_TILE
    pairs = FOX_HEADS // 2
    return pl.pallas_call(
        _fox_kernel,
        grid=(pairs, T // t),
        in_specs=[
            pl.BlockSpec((t, LANES), lambda p, i: (i, p)),
            pl.BlockSpec((LANES, T), lambda p, i: (p, 0)),
            pl.BlockSpec((T, LANES), lambda p, i: (0, p)),
            pl.BlockSpec((1, t, 2), lambda p, i: (p, i, 0)),
            pl.BlockSpec((1, 2, T), lambda p, i: (p, 0, 0)),
        ],
        out_specs=pl.BlockSpec((t, LANES), lambda p, i: (i, p)),
        out_shape=jax.ShapeDtypeStruct((T, FOX_HEADS * FOX_HEAD_DIM), BF16),
        scratch_shapes=[
            pltpu.VMEM((2, t, LANES), F32),
            pltpu.VMEM((2, t, 1), F32),
            pltpu.VMEM((2, t, 1), F32),
        ],
        compiler_params=_cparams(("parallel", "arbitrary")),
        name="fox",
    )(q, kt, v, c_col, c_row)


def kernel(x, norm_mix, norm_mlp, norm_final, w_up, w_down, swa_w_qkv, swa_b_qkv, swa_sinks, swa_w_o,
           hgrn_w_in, hgrn_lb_logits, hgrn_g_norm, hgrn_w_o, fox_w_in, fox_b_in, fox_w_o):
    B, T, D = x.shape
    depth = norm_mix.shape[0]
    lb_soft = jax.nn.softmax(hgrn_lb_logits.astype(F32), axis=0)
    lower_bounds = jnp.cumsum(lb_soft, axis=0) - lb_soft[0]

    outs = []
    for b in range(B):
        xb = x[b]
        for i in range(depth):
            m, j = i % N_MIXERS, i // N_MIXERS
            if m == 0:
                nq = SWA_HEADS * SWA_HEAD_DIM
                nkv = SWA_KV_HEADS * SWA_HEAD_DIM
                q, k, v = norm_proj(xb, norm_mix[i], swa_w_qkv[j].astype(BF16), swa_b_qkv[j],
                                    [(0, nq), (nq, nq + nkv), (nq + nkv, nq + 2 * nkv)], [BF16] * 3, "swa_proj")
                a = swa_attention(q, k, v, swa_sinks[j])
                xb = proj_residual(a, swa_w_o[j].astype(BF16), xb, "swa_out")
            elif m == 1:
                hk = HGRN_HEADS * HGRN_EXPAND
                hv = HGRN_HEADS * HGRN_HEAD_DIM
                q, f, v, g = norm_proj(xb, norm_mix[i], hgrn_w_in[j].astype(BF16), jnp.zeros((2 * hk + 2 * hv,), F32),
                                       [(0, hk), (hk, 2 * hk), (2 * hk, 2 * hk + hv), (2 * hk + hv, 2 * hk + 2 * hv)],
                                       [F32, F32, BF16, F32], "hgrn_proj")
                o = hgrn_recurrence(q, f, v, lower_bounds[i])
                xb = hgrn_out(o, g, hgrn_g_norm[j], hgrn_w_o[j].astype(BF16), xb)
            else:
                n = FOX_HEADS * FOX_HEAD_DIM
                pad = LANES - FOX_HEADS
                w = jnp.pad(fox_w_in[j], ((0, 0), (0, pad))).astype(BF16)
                bias = jnp.pad(fox_b_in[j], ((0, pad),))
                q, k, v, f = norm_proj(xb, norm_mix[i], w, bias,
                                       [(0, n), (n, 2 * n), (2 * n, 3 * n), (3 * n, 3 * n + LANES)],
                                       [BF16, BF16, BF16, F32], "fox_proj")
                c = fox_cumsum(f)[:, :FOX_HEADS]
                c_col = c.reshape(T, FOX_HEADS // 2, 2).transpose(1, 0, 2)
                c_row = c.T.reshape(FOX_HEADS // 2, 2, T)
                a = fox_attention(q, k.T, v, c_col, c_row)
                xb = proj_residual(a, fox_w_o[j].astype(BF16), xb, "fox_out")
            last = i == depth - 1
            xb = mlp(xb, norm_mlp[i], w_up[i].astype(BF16), w_down[i].astype(BF16), norm_final, last)
        outs.append(xb)
    return jnp.stack(outs, axis=0)
```

```python
import functools

import jax
import jax.numpy as jnp
from jax import lax
from jax.experimental import pallas as pl
from jax.experimental.pallas import tpu as pltpu

F32 = jnp.float32
BF16 = jnp.bfloat16

D_MODEL = 1024
D_FF = 4 * D_MODEL
EPS = 1e-6
N_MIXERS = 3

SWA_HEADS = 16
SWA_KV_HEADS = 4
SWA_HEAD_DIM = 64
SWA_WINDOW = 128

HGRN_HEADS = 8
HGRN_EXPAND = 128
HGRN_HEAD_DIM = 128
HGRN_CHUNK = 64
HGRN_SUB = 16

FOX_HEADS = 16
FOX_HEAD_DIM = 64

LANES = 128
VMEM_LIMIT = 56 * 1024 * 1024
NEG_BIG = -1e30
LOG2E = 1.4426950408889634

ROW_TILE = 512


def _cparams(sem):
    return pltpu.CompilerParams(dimension_semantics=sem, vmem_limit_bytes=VMEM_LIMIT)


def _const_spec(shape):
    nd = len(shape)
    return pl.BlockSpec(shape, lambda *_: (0,) * nd, pipeline_mode=pl.Buffered(1))


def _rms(x, g):
    ms = jnp.mean(x * x, axis=-1, keepdims=True)
    return x * lax.rsqrt(ms + EPS) * g


def _norm_proj_kernel(x_ref, g_ref, w_ref, b_ref, *o_refs, segments, n_chunk):
    h = _rms(x_ref[...], g_ref[...]).astype(BF16)
    for o_ref, (lo, hi) in zip(o_refs, segments):
        for j in range(lo, hi, n_chunk):
            je = min(j + n_chunk, hi)
            y = jnp.dot(h, w_ref[:, j:je], preferred_element_type=F32) + b_ref[:, j:je]
            o_ref[:, j - lo:je - lo] = y.astype(o_ref.dtype)


def norm_proj(x, g, w, b, segments, dtypes, name):
    T, D = x.shape
    N = w.shape[1]
    tm = ROW_TILE
    out_shape = [jax.ShapeDtypeStruct((T, hi - lo), dt) for (lo, hi), dt in zip(segments, dtypes)]
    out_specs = [pl.BlockSpec((tm, hi - lo), lambda i: (i, 0)) for (lo, hi) in segments]
    return pl.pallas_call(
        functools.partial(_norm_proj_kernel, segments=tuple(segments), n_chunk=512),
        grid=(T // tm,),
        in_specs=[
            pl.BlockSpec((tm, D), lambda i: (i, 0)),
            _const_spec((1, D)),
            _const_spec((D, N)),
            _const_spec((1, N)),
        ],
        out_specs=out_specs,
        out_shape=out_shape,
        compiler_params=_cparams(("parallel",)),
        name=name,
    )(x, g.reshape(1, D), w, b.reshape(1, N))


def _proj_residual_kernel(a_ref, w_ref, x_ref, o_ref):
    a = a_ref[...].astype(BF16)
    o_ref[...] = x_ref[...] + jnp.dot(a, w_ref[...], preferred_element_type=F32)


def proj_residual(a, w, x, name):
    T, K = a.shape
    D = w.shape[1]
    tm = ROW_TILE
    return pl.pallas_call(
        _proj_residual_kernel,
        grid=(T // tm,),
        in_specs=[
            pl.BlockSpec((tm, K), lambda i: (i, 0)),
            _const_spec((K, D)),
            pl.BlockSpec((tm, D), lambda i: (i, 0)),
        ],
        out_specs=pl.BlockSpec((tm, D), lambda i: (i, 0)),
        out_shape=jax.ShapeDtypeStruct((T, D), F32),
        compiler_params=_cparams(("parallel",)),
        name=name,
    )(a, w, x)


def _hgrn_out_kernel(o_ref, g_ref, gn_ref, w_ref, x_ref, y_ref):
    parts = []
    for h in range(HGRN_HEADS):
        sl = slice(h * HGRN_HEAD_DIM, (h + 1) * HGRN_HEAD_DIM)
        o = o_ref[:, sl]
        o = o * lax.rsqrt(jnp.mean(o * o, axis=-1, keepdims=True) + EPS)
        g = g_ref[:, sl]
        parts.append((o * gn_ref[:, sl] * (g * jax.nn.sigmoid(g))).astype(BF16))
    a = jnp.concatenate(parts, axis=1)
    y_ref[...] = x_ref[...] + jnp.dot(a, w_ref[...], preferred_element_type=F32)


def hgrn_out(o, g, g_norm, w, x):
    T, D = x.shape
    tm = ROW_TILE
    row = lambda i: (i, 0)
    return pl.pallas_call(
        _hgrn_out_kernel,
        grid=(T // tm,),
        in_specs=[
            pl.BlockSpec((tm, D), row),
            pl.BlockSpec((tm, D), row),
            _const_spec((1, D)),
            _const_spec((D, D)),
            pl.BlockSpec((tm, D), row),
        ],
        out_specs=pl.BlockSpec((tm, D), row),
        out_shape=jax.ShapeDtypeStruct((T, D), F32),
        compiler_params=_cparams(("parallel",)),
        name="hgrn_out",
    )(o, g, g_norm.reshape(1, D), w, x)


def _mlp_kernel(x_ref, g_ref, wu_ref, wd_ref, gf_ref, o_ref, *, ff_chunk, final_norm):
    x = x_ref[...]
    h = _rms(x, g_ref[...]).astype(BF16)
    acc = x
    for j in range(0, D_FF, ff_chunk):
        u = jnp.dot(h, wu_ref[:, j:j + ff_chunk], preferred_element_type=F32)
        u = jnp.maximum(u, 0.0)
        u = (u * u).astype(BF16)
        acc = acc + jnp.dot(u, wd_ref[j:j + ff_chunk, :], preferred_element_type=F32)
    if final_norm:
        acc = _rms(acc, gf_ref[...])
    o_ref[...] = acc


def mlp(x, g, w_up, w_down, g_final, final_norm):
    T, D = x.shape
    tm = ROW_TILE
    return pl.pallas_call(
        functools.partial(_mlp_kernel, ff_chunk=512, final_norm=final_norm),
        grid=(T // tm,),
        in_specs=[
            pl.BlockSpec((tm, D), lambda i: (i, 0)),
            _const_spec((1, D)),
            _const_spec((D, D_FF)),
            _const_spec((D_FF, D)),
            _const_spec((1, D)),
        ],
        out_specs=pl.BlockSpec((tm, D), lambda i: (i, 0)),
        out_shape=jax.ShapeDtypeStruct((T, D), F32),
        compiler_params=_cparams(("parallel",)),
        name="mlp",
    )(x, g.reshape(1, D), w_up, w_down, g_final.reshape(1, D))


def _swa_kernel(sink_ref, q_ref, kc_ref, kp_ref, vc_ref, vp_ref, o_ref):
    n = pl.program_id(0)
    W, dh = SWA_WINDOW, SWA_HEAD_DIM
    G = SWA_HEADS // SWA_KV_HEADS
    qi = lax.broadcasted_iota(jnp.int32, (W, 2 * W), 0)
    kk = lax.broadcasted_iota(jnp.int32, (W, 2 * W), 1)
    rel = qi + W - kk
    visible = (rel >= 0) & (rel < W) & ((n > 0) | (kk >= W))
    outs = []
    for hk in range(SWA_KV_HEADS):
        ks = slice(hk * dh, (hk + 1) * dh)
        kb = jnp.concatenate([kp_ref[:, ks], kc_ref[:, ks]], axis=0)
        vb = jnp.concatenate([vp_ref[:, ks], vc_ref[:, ks]], axis=0)
        for g in range(G):
            head = hk * G + g
            q = q_ref[:, head * dh:(head + 1) * dh] * (dh ** -0.5)
            s = lax.dot_general(q, kb, (((1,), (1,)), ((), ())), preferred_element_type=F32)
            s = jnp.where(visible, s, NEG_BIG)
            sink = sink_ref[head]
            m = jnp.maximum(jnp.max(s, axis=-1, keepdims=True), sink)
            p = jnp.exp(s - m)
            denom = jnp.sum(p, axis=-1, keepdims=True) + jnp.exp(sink - m)
            o = jnp.dot(p.astype(BF16), vb, preferred_element_type=F32)
            outs.append(o / denom)
    o_ref[...] = jnp.concatenate(outs, axis=1).astype(o_ref.dtype)


def swa_attention(q, k, v, sinks):
    T = q.shape[0]
    W = SWA_WINDOW
    kvw = SWA_KV_HEADS * SWA_HEAD_DIM
    cur = lambda n: (n, 0)
    prev = lambda n: (jnp.maximum(n - 1, 0), 0)
    return pl.pallas_call(
        _swa_kernel,
        grid=(T // W,),
        in_specs=[
            pl.BlockSpec(memory_space=pltpu.SMEM),
            pl.BlockSpec((W, SWA_HEADS * SWA_HEAD_DIM), cur),
            pl.BlockSpec((W, kvw), cur),
            pl.BlockSpec((W, kvw), prev),
            pl.BlockSpec((W, kvw), cur),
            pl.BlockSpec((W, kvw), prev),
        ],
        out_specs=pl.BlockSpec((W, SWA_HEADS * SWA_HEAD_DIM), cur),
        out_shape=jax.ShapeDtypeStruct((T, SWA_HEADS * SWA_HEAD_DIM), BF16),
        compiler_params=_cparams(("parallel",)),
        name="swa",
    )(sinks, q, k, k, v, v)


def _split_dot(m_bf16, x):
    hi = x.astype(BF16)
    lo = (x - hi.astype(F32)).astype(BF16)
    return (jnp.dot(m_bf16, hi, preferred_element_type=F32)
            + jnp.dot(m_bf16, lo, preferred_element_type=F32))


def _hgrn_kernel(q_ref, f_ref, v_ref, lb_ref, o_ref, st_ref):
    C, SUB = HGRN_CHUNK, HGRN_SUB
    K = HGRN_EXPAND

    @pl.when(pl.program_id(1) == 0)
    def _():
        st_ref[...] = jnp.zeros_like(st_ref)

    lb = lb_ref[...]
    f = lb + (1.0 - lb) * jax.nn.sigmoid(f_ref[...])
    logf = jnp.log(f)
    kk = 1.0 - f
    q = q_ref[...]
    qs = q * jax.nn.sigmoid(q)
    v = v_ref[...]

    row = lax.broadcasted_iota(jnp.int32, (C, C), 0)
    col = lax.broadcasted_iota(jnp.int32, (C, C), 1)
    tril = (row >= col)
    same_sub = (row // SUB) == (col // SUB)
    bcum = _split_dot(tril.astype(BF16), logf)
    bsub = _split_dot((tril & same_sub).astype(BF16), logf)
    b_last = bcum[C - 1:C, :]

    st = st_ref[...]
    q_in = (qs * jnp.exp(bcum)).astype(BF16)
    o = lax.dot_general(q_in, st.astype(BF16), (((1,), (1,)), ((), ())), preferred_element_type=F32)
    k_out = (kk * jnp.exp(b_last - bcum)).astype(BF16)
    upd = lax.dot_general(v, k_out, (((0,), (0,)), ((), ())), preferred_element_type=F32)
    st_ref[...] = st * jnp.exp(b_last) + upd

    q_sub = (qs * jnp.exp(bsub)).astype(BF16)
    a = jnp.zeros((C, C), F32)
    for i in range(1, C // SUB):
        r_i = bcum[i * SUB - 1:i * SUB, :]
        k_i = (kk * jnp.exp(jnp.minimum(r_i - bcum, 0.0))).astype(BF16)
        a_i = lax.dot_general(q_sub, k_i, (((1,), (1,)), ((), ())), preferred_element_type=F32)
        a = jnp.where((row // SUB == i) & (col < i * SUB), a_i, a)
    o = o + jnp.dot(a.astype(BF16), v, preferred_element_type=F32)

    sub_pos = lax.broadcasted_iota(jnp.int32, (C, K), 0) % SUB
    ones = jnp.ones((K, LANES), BF16)
    vf = v.astype(F32)
    for d in range(SUB):
        if d == 0:
            e = qs * kk
            vd = vf
        else:
            e = qs * pltpu.roll(kk, d, 0) * jnp.exp(bcum - pltpu.roll(bcum, d, 0))
            e = jnp.where(sub_pos >= d, e, 0.0)
            vd = pltpu.roll(vf, d, 0)
        rs = jnp.dot(e.astype(BF16), ones, preferred_element_type=F32)
        o = o + rs * vd
    o_ref[...] = o


def hgrn_recurrence(q, f_logit, v, lb):
    T = q.shape[0]
    C = HGRN_CHUNK
    blk = lambda h, c: (c, h)
    return pl.pallas_call(
        _hgrn_kernel,
        grid=(HGRN_HEADS, T // C),
        in_specs=[
            pl.BlockSpec((C, HGRN_EXPAND), blk),
            pl.BlockSpec((C, HGRN_EXPAND), blk),
            pl.BlockSpec((C, HGRN_HEAD_DIM), blk),
            pl.BlockSpec((1, HGRN_EXPAND), lambda h, c: (0, h)),
        ],
        out_specs=pl.BlockSpec((C, HGRN_HEAD_DIM), blk),
        out_shape=jax.ShapeDtypeStruct((T, HGRN_HEADS * HGRN_HEAD_DIM), F32),
        scratch_shapes=[pltpu.VMEM((HGRN_HEAD_DIM, HGRN_EXPAND), F32)],
        compiler_params=_cparams(("parallel", "arbitrary")),
        name="hgrn",
    )(q, f_logit, v, lb.reshape(1, -1))


FOX_CUM_TILE = 256


def _split3(x):
    hi = x.astype(BF16)
    r1 = x - hi.astype(F32)
    mid = r1.astype(BF16)
    lo = (r1 - mid.astype(F32)).astype(BF16)
    return hi, mid, lo


def _fox_cumsum_kernel(f_ref, hi_ref, mid_ref, lo_ref, carry_ref):
    @pl.when(pl.program_id(0) == 0)
    def _():
        carry_ref[...] = jnp.zeros_like(carry_ref)

    x = f_ref[...]
    ls = jnp.minimum(x, 0.0) - jnp.log(1.0 + jnp.exp(-jnp.abs(x)))
    n = FOX_CUM_TILE
    tril = (lax.broadcasted_iota(jnp.int32, (n, n), 0) >= lax.broadcasted_iota(jnp.int32, (n, n), 1)).astype(BF16)
    hi, mid, lo = _split3(ls)
    cum = (jnp.dot(tril, hi, preferred_element_type=F32)
           + jnp.dot(tril, mid, preferred_element_type=F32)
           + jnp.dot(tril, lo, preferred_element_type=F32))
    c = cum + carry_ref[...]
    carry_ref[...] = c[n - 1:n, :]
    hi2, mid2, lo2 = _split3(c * LOG2E)
    hi_ref[...] = hi2
    mid_ref[...] = mid2
    lo_ref[...] = lo2


def fox_cumsum(f_logit):
    T = f_logit.shape[0]
    n = FOX_CUM_TILE
    spec = pl.BlockSpec((n, LANES), lambda i: (i, 0))
    return pl.pallas_call(
        _fox_cumsum_kernel,
        grid=(T // n,),
        in_specs=[spec],
        out_specs=[spec] * 3,
        out_shape=[jax.ShapeDtypeStruct((T, LANES), BF16)] * 3,
        scratch_shapes=[pltpu.VMEM((1, LANES), F32)],
        compiler_params=_cparams(("arbitrary",)),
        name="fox_cumsum",
    )(f_logit)


FOX_TILE = 512
FOX_KEY_TILE = 256


FOX_VROWS = FOX_HEAD_DIM + 16


def _fox_kernel(k_ref, qt_ref, vt_ref, o_ref, acc_ref, m_ref, alpha_ref, st_ref, pt_ref):
    i = pl.program_id(1)
    tq, tk = FOX_TILE, FOX_KEY_TILE
    dh = FOX_HEAD_DIM

    acc_ref[...] = jnp.zeros_like(acc_ref)
    m_ref[...] = jnp.full_like(m_ref, NEG_BIG)
    pt_ref[1] = jnp.zeros(pt_ref.shape[1:], BF16)
    alpha_ref[1] = jnp.ones(alpha_ref.shape[1:], F32)

    def logits(j, slot):
        start = pl.multiple_of(j * tk, tk)
        for h in range(2):
            hs = slice(h * LANES, (h + 1) * LANES)
            st_ref[slot, h] = jnp.dot(k_ref[pl.ds(start, tk), hs], qt_ref[hs, :], preferred_element_type=F32)

    def softmax(slot, mask_offset=None):
        for h in range(2):
            st = st_ref[slot, h]
            if mask_offset is not None:
                causal = (lax.broadcasted_iota(jnp.int32, (tk, tq), 0) + mask_offset
                          <= lax.broadcasted_iota(jnp.int32, (tk, tq), 1))
                st = jnp.where(causal, st, NEG_BIG)
            m_old = m_ref[h]
            m_new = jnp.maximum(m_old, jnp.max(st, axis=0, keepdims=True))
            alpha_ref[slot, h] = jnp.exp2(m_old - m_new)
            pt_ref[slot, h] = jnp.exp2(st - m_new).astype(BF16)
            m_ref[h] = m_new

    def accum(j, slot):
        start = pl.multiple_of(j * tk, tk)
        for h in range(2):
            vs = slice(h * FOX_VROWS, (h + 1) * FOX_VROWS)
            pv = jnp.dot(vt_ref[vs, pl.ds(start, tk)], pt_ref[slot, h], preferred_element_type=F32)
            acc_ref[h] = alpha_ref[slot, h] * acc_ref[h] + pv

    sub = tq // tk

    def body(p, carry):
        accum(jnp.maximum(sub * p - 1, 0), 1)
        softmax(0)
        logits(sub * p + 1, 1)
        accum(sub * p, 0)
        softmax(1)
        logits(sub * p + 2, 0)
        return carry

    logits(0, 0)
    lax.fori_loop(0, i, body, 0)
    accum(jnp.maximum(sub * i - 1, 0), 1)
    softmax(0, mask_offset=0)
    logits(sub * i + 1, 1)
    accum(sub * i, 0)
    softmax(1, mask_offset=tk)
    accum(sub * i + 1, 1)

    acc_a, acc_b = acc_ref[0], acc_ref[1]
    ot = jnp.concatenate([acc_a[:dh] / acc_a[dh:dh + 1], acc_b[:dh] / acc_b[dh:dh + 1]], axis=0)
    o_ref[...] = ot.T.astype(o_ref.dtype)


def fox_attention(k_aug, qt_aug, vt_aug):
    T = k_aug.shape[0]
    t = FOX_TILE
    pairs = FOX_HEADS // 2
    return pl.pallas_call(
        _fox_kernel,
        grid=(pairs, T // t),
        in_specs=[
            pl.BlockSpec((T, 2 * LANES), lambda p, i: (0, p)),
            pl.BlockSpec((2 * LANES, t), lambda p, i: (p, i)),
            pl.BlockSpec((2 * FOX_VROWS, T), lambda p, i: (p, 0)),
        ],
        out_specs=pl.BlockSpec((t, LANES), lambda p, i: (i, p)),
        out_shape=jax.ShapeDtypeStruct((T, FOX_HEADS * FOX_HEAD_DIM), BF16),
        scratch_shapes=[
            pltpu.VMEM((2, FOX_VROWS, t), F32),
            pltpu.VMEM((2, 1, t), F32),
            pltpu.VMEM((2, 2, 1, t), F32),
            pltpu.VMEM((2, 2, FOX_KEY_TILE, t), F32),
            pltpu.VMEM((2, 2, FOX_KEY_TILE, t), BF16),
        ],
        compiler_params=_cparams(("parallel", "arbitrary")),
        name="fox",
    )(k_aug, qt_aug, vt_aug)


def fox_augment(q, k, v, hi, mid, lo):
    T = q.shape[0]
    H, dh = FOX_HEADS, FOX_HEAD_DIM
    pad = LANES - dh - 6
    c3 = [a[:, :H].reshape(T, H, 1) for a in (hi, mid, lo)]
    ones3 = jnp.ones((T, H, 3), BF16)
    zeros = jnp.zeros((T, H, pad), BF16)
    q_aug = jnp.concatenate([q.reshape(T, H, dh)] + c3 + [ones3, zeros], axis=-1)
    qt_aug = q_aug.transpose(1, 2, 0).reshape(H * LANES, T)
    k_aug = jnp.concatenate([k.reshape(T, H, dh), ones3] + [-a for a in c3] + [zeros], axis=-1).reshape(T, H * LANES)
    vt = v.reshape(T, H, dh).transpose(1, 2, 0)
    vt_aug = jnp.concatenate([vt, jnp.ones((H, FOX_VROWS - dh, T), BF16)], axis=1).reshape(H * FOX_VROWS, T)
    return k_aug, qt_aug, vt_aug


def kernel(x, norm_mix, norm_mlp, norm_final, w_up, w_down, swa_w_qkv, swa_b_qkv, swa_sinks, swa_w_o,
           hgrn_w_in, hgrn_lb_logits, hgrn_g_norm, hgrn_w_o, fox_w_in, fox_b_in, fox_w_o):
    B, T, D = x.shape
    depth = norm_mix.shape[0]
    lb_soft = jax.nn.softmax(hgrn_lb_logits.astype(F32), axis=0)
    lower_bounds = jnp.cumsum(lb_soft, axis=0) - lb_soft[0]

    outs = []
    for b in range(B):
        xb = x[b]
        for i in range(depth):
            m, j = i % N_MIXERS, i // N_MIXERS
            if m == 0:
                nq = SWA_HEADS * SWA_HEAD_DIM
                nkv = SWA_KV_HEADS * SWA_HEAD_DIM
                q, k, v = norm_proj(xb, norm_mix[i], swa_w_qkv[j].astype(BF16), swa_b_qkv[j],
                                    [(0, nq), (nq, nq + nkv), (nq + nkv, nq + 2 * nkv)], [BF16] * 3, "swa_proj")
                a = swa_attention(q, k, v, swa_sinks[j])
                xb = proj_residual(a, swa_w_o[j].astype(BF16), xb, "swa_out")
            elif m == 1:
                hk = HGRN_HEADS * HGRN_EXPAND
                hv = HGRN_HEADS * HGRN_HEAD_DIM
                q, f, v, g = norm_proj(xb, norm_mix[i], hgrn_w_in[j].astype(BF16), jnp.zeros((2 * hk + 2 * hv,), F32),
                                       [(0, hk), (hk, 2 * hk), (2 * hk, 2 * hk + hv), (2 * hk + hv, 2 * hk + 2 * hv)],
                                       [F32, F32, BF16, F32], "hgrn_proj")
                o = hgrn_recurrence(q, f, v, lower_bounds[i])
                xb = hgrn_out(o, g, hgrn_g_norm[j], hgrn_w_o[j].astype(BF16), xb)
            else:
                n = FOX_HEADS * FOX_HEAD_DIM
                pad = LANES - FOX_HEADS
                col_scale = jnp.concatenate([jnp.full((n,), LOG2E * FOX_HEAD_DIM ** -0.5, F32),
                                             jnp.ones((2 * n + FOX_HEADS,), F32)])
                w = jnp.pad(fox_w_in[j] * col_scale, ((0, 0), (0, pad))).astype(BF16)
                bias = jnp.pad(fox_b_in[j] * col_scale, ((0, pad),))
                q, k, v, f = norm_proj(xb, norm_mix[i], w, bias,
                                       [(0, n), (n, 2 * n), (2 * n, 3 * n), (3 * n, 3 * n + LANES)],
                                       [BF16, BF16, BF16, F32], "fox_proj")
                hi, mid, lo = fox_cumsum(f)
                a = fox_attention(*fox_augment(q, k, v, hi, mid, lo))
                xb = proj_residual(a, fox_w_o[j].astype(BF16), xb, "fox_out")
            last = i == depth - 1
            xb = mlp(xb, norm_mlp[i], w_up[i].astype(BF16), w_down[i].astype(BF16), norm_final, last)
        outs.append(xb)
    return jnp.stack(outs, axis=0)
```

```python
import functools

import jax
import jax.numpy as jnp
from jax import lax
from jax.experimental import pallas as pl
from jax.experimental.pallas import tpu as pltpu

F32 = jnp.float32
BF16 = jnp.bfloat16

D_MODEL = 1024
D_FF = 4 * D_MODEL
EPS = 1e-6
N_MIXERS = 3

SWA_HEADS = 16
SWA_KV_HEADS = 4
SWA_HEAD_DIM = 64
SWA_WINDOW = 128

HGRN_HEADS = 8
HGRN_EXPAND = 128
HGRN_HEAD_DIM = 128
HGRN_CHUNK = 64
HGRN_SUB = 8

FOX_HEADS = 16
FOX_HEAD_DIM = 64

LANES = 128
VMEM_LIMIT = 56 * 1024 * 1024
NEG_BIG = -1e30
LOG2E = 1.4426950408889634

ROW_TILE = 512


def _cparams(sem):
    return pltpu.CompilerParams(dimension_semantics=sem, vmem_limit_bytes=VMEM_LIMIT)


def _const_spec(shape):
    nd = len(shape)
    return pl.BlockSpec(shape, lambda *_: (0,) * nd, pipeline_mode=pl.Buffered(1))


def _rms(x, g):
    ms = jnp.mean(x * x, axis=-1, keepdims=True)
    return x * lax.rsqrt(ms + EPS) * g


def _norm_proj_kernel(x_ref, g_ref, w_ref, b_ref, *o_refs, segments, transposed, n_chunk):
    h = _rms(x_ref[...], g_ref[...]).astype(BF16)
    for o_ref, (lo, hi), tr in zip(o_refs, segments, transposed):
        for j in range(lo, hi, n_chunk):
            je = min(j + n_chunk, hi)
            y = jnp.dot(h, w_ref[:, j:je], preferred_element_type=F32) + b_ref[:, j:je]
            if tr:
                o_ref[j - lo:je - lo, :] = y.T.astype(o_ref.dtype)
            else:
                o_ref[:, j - lo:je - lo] = y.astype(o_ref.dtype)


def norm_proj(x, g, w, b, segments, dtypes, name, transposed=None):
    T, D = x.shape
    N = w.shape[1]
    tm = ROW_TILE
    transposed = tuple(transposed) if transposed is not None else (False,) * len(segments)
    out_shape, out_specs = [], []
    for (lo, hi), dt, tr in zip(segments, dtypes, transposed):
        if tr:
            out_shape.append(jax.ShapeDtypeStruct((hi - lo, T), dt))
            out_specs.append(pl.BlockSpec((hi - lo, tm), lambda i: (0, i)))
        else:
            out_shape.append(jax.ShapeDtypeStruct((T, hi - lo), dt))
            out_specs.append(pl.BlockSpec((tm, hi - lo), lambda i: (i, 0)))
    return pl.pallas_call(
        functools.partial(_norm_proj_kernel, segments=tuple(segments), transposed=transposed, n_chunk=512),
        grid=(T // tm,),
        in_specs=[
            pl.BlockSpec((tm, D), lambda i: (i, 0)),
            _const_spec((1, D)),
            _const_spec((D, N)),
            _const_spec((1, N)),
        ],
        out_specs=out_specs,
        out_shape=out_shape,
        compiler_params=_cparams(("parallel",)),
        name=name,
    )(x, g.reshape(1, D), w, b.reshape(1, N))


def _proj_residual_kernel(a_ref, w_ref, x_ref, o_ref):
    a = a_ref[...].astype(BF16)
    o_ref[...] = x_ref[...] + jnp.dot(a, w_ref[...], preferred_element_type=F32)


def proj_residual(a, w, x, name):
    T, K = a.shape
    D = w.shape[1]
    tm = ROW_TILE
    return pl.pallas_call(
        _proj_residual_kernel,
        grid=(T // tm,),
        in_specs=[
            pl.BlockSpec((tm, K), lambda i: (i, 0)),
            _const_spec((K, D)),
            pl.BlockSpec((tm, D), lambda i: (i, 0)),
        ],
        out_specs=pl.BlockSpec((tm, D), lambda i: (i, 0)),
        out_shape=jax.ShapeDtypeStruct((T, D), F32),
        compiler_params=_cparams(("parallel",)),
        name=name,
    )(a, w, x)


def _hgrn_out_kernel(o_ref, g_ref, gn_ref, w_ref, x_ref, y_ref):
    parts = []
    for h in range(HGRN_HEADS):
        sl = slice(h * HGRN_HEAD_DIM, (h + 1) * HGRN_HEAD_DIM)
        o = o_ref[:, sl]
        o = o * lax.rsqrt(jnp.mean(o * o, axis=-1, keepdims=True) + EPS)
        g = g_ref[:, sl]
        parts.append((o * gn_ref[:, sl] * (g * jax.nn.sigmoid(g))).astype(BF16))
    a = jnp.concatenate(parts, axis=1)
    y_ref[...] = x_ref[...] + jnp.dot(a, w_ref[...], preferred_element_type=F32)


def hgrn_out(o, g, g_norm, w, x):
    T, D = x.shape
    tm = ROW_TILE
    row = lambda i: (i, 0)
    return pl.pallas_call(
        _hgrn_out_kernel,
        grid=(T // tm,),
        in_specs=[
            pl.BlockSpec((tm, D), row),
            pl.BlockSpec((tm, D), row),
            _const_spec((1, D)),
            _const_spec((D, D)),
            pl.BlockSpec((tm, D), row),
        ],
        out_specs=pl.BlockSpec((tm, D), row),
        out_shape=jax.ShapeDtypeStruct((T, D), F32),
        compiler_params=_cparams(("parallel",)),
        name="hgrn_out",
    )(o, g, g_norm.reshape(1, D), w, x)


def _mlp_kernel(x_ref, g_ref, wu_ref, wd_ref, gf_ref, o_ref, *, ff_chunk, final_norm):
    x = x_ref[...]
    h = _rms(x, g_ref[...]).astype(BF16)
    acc = x
    for j in range(0, D_FF, ff_chunk):
        u = jnp.dot(h, wu_ref[:, j:j + ff_chunk], preferred_element_type=F32)
        u = jnp.maximum(u, 0.0)
        u = (u * u).astype(BF16)
        acc = acc + jnp.dot(u, wd_ref[j:j + ff_chunk, :], preferred_element_type=F32)
    if final_norm:
        acc = _rms(acc, gf_ref[...])
    o_ref[...] = acc


def mlp(x, g, w_up, w_down, g_final, final_norm):
    T, D = x.shape
    tm = ROW_TILE
    return pl.pallas_call(
        functools.partial(_mlp_kernel, ff_chunk=512, final_norm=final_norm),
        grid=(T // tm,),
        in_specs=[
            pl.BlockSpec((tm, D), lambda i: (i, 0)),
            _const_spec((1, D)),
            _const_spec((D, D_FF)),
            _const_spec((D_FF, D)),
            _const_spec((1, D)),
        ],
        out_specs=pl.BlockSpec((tm, D), lambda i: (i, 0)),
        out_shape=jax.ShapeDtypeStruct((T, D), F32),
        compiler_params=_cparams(("parallel",)),
        name="mlp",
    )(x, g.reshape(1, D), w_up, w_down, g_final.reshape(1, D))


def _swa_kernel(sink_ref, q_ref, kc_ref, kp_ref, vc_ref, vp_ref, o_ref):
    n = pl.program_id(0)
    W, dh = SWA_WINDOW, SWA_HEAD_DIM
    G = SWA_HEADS // SWA_KV_HEADS
    qi = lax.broadcasted_iota(jnp.int32, (W, 2 * W), 0)
    kk = lax.broadcasted_iota(jnp.int32, (W, 2 * W), 1)
    rel = qi + W - kk
    visible = (rel >= 0) & (rel < W) & ((n > 0) | (kk >= W))
    outs = []
    for hk in range(SWA_KV_HEADS):
        ks = slice(hk * dh, (hk + 1) * dh)
        kb = jnp.concatenate([kp_ref[:, ks], kc_ref[:, ks]], axis=0)
        vb = jnp.concatenate([vp_ref[:, ks], vc_ref[:, ks]], axis=0)
        for g in range(G):
            head = hk * G + g
            q = q_ref[:, head * dh:(head + 1) * dh] * (dh ** -0.5)
            s = lax.dot_general(q, kb, (((1,), (1,)), ((), ())), preferred_element_type=F32)
            s = jnp.where(visible, s, NEG_BIG)
            sink = sink_ref[head]
            m = jnp.maximum(jnp.max(s, axis=-1, keepdims=True), sink)
            p = jnp.exp(s - m)
            denom = jnp.sum(p, axis=-1, keepdims=True) + jnp.exp(sink - m)
            o = jnp.dot(p.astype(BF16), vb, preferred_element_type=F32)
            outs.append(o / denom)
    o_ref[...] = jnp.concatenate(outs, axis=1).astype(o_ref.dtype)


def swa_attention(q, k, v, sinks):
    T = q.shape[0]
    W = SWA_WINDOW
    kvw = SWA_KV_HEADS * SWA_HEAD_DIM
    cur = lambda n: (n, 0)
    prev = lambda n: (jnp.maximum(n - 1, 0), 0)
    return pl.pallas_call(
        _swa_kernel,
        grid=(T // W,),
        in_specs=[
            pl.BlockSpec(memory_space=pltpu.SMEM),
            pl.BlockSpec((W, SWA_HEADS * SWA_HEAD_DIM), cur),
            pl.BlockSpec((W, kvw), cur),
            pl.BlockSpec((W, kvw), prev),
            pl.BlockSpec((W, kvw), cur),
            pl.BlockSpec((W, kvw), prev),
        ],
        out_specs=pl.BlockSpec((W, SWA_HEADS * SWA_HEAD_DIM), cur),
        out_shape=jax.ShapeDtypeStruct((T, SWA_HEADS * SWA_HEAD_DIM), BF16),
        compiler_params=_cparams(("parallel",)),
        name="swa",
    )(sinks, q, k, k, v, v)


def _split_dot(m_bf16, x):
    hi = x.astype(BF16)
    lo = (x - hi.astype(F32)).astype(BF16)
    return (jnp.dot(m_bf16, hi, preferred_element_type=F32)
            + jnp.dot(m_bf16, lo, preferred_element_type=F32))


def _hgrn_kernel(q_ref, f_ref, v_ref, lb_ref, o_ref, st_ref):
    C, SUB, K, H = HGRN_CHUNK, HGRN_SUB, HGRN_EXPAND, HGRN_HEADS
    nt = (((1,), (1,)), ((), ()))
    tn = (((0,), (0,)), ((), ()))

    @pl.when(pl.program_id(0) == 0)
    def _():
        st_ref[...] = jnp.zeros_like(st_ref)

    row = lax.broadcasted_iota(jnp.int32, (C, C), 0)
    col = lax.broadcasted_iota(jnp.int32, (C, C), 1)
    tril = row >= col
    row_sub = row // SUB
    heads = [slice(h * K, (h + 1) * K) for h in range(H)]

    lb = lb_ref[...]
    f = lb + (1.0 - lb) * jax.nn.sigmoid(f_ref[...])
    logf = jnp.log2(f)
    kk = 1.0 - f
    q = q_ref[...]
    qs = q * jax.nn.sigmoid(q)
    v = v_ref[...]
    bcum = _split_dot(tril.astype(BF16), logf)
    bsub = _split_dot((tril & (row_sub == col // SUB)).astype(BF16), logf)
    b_last = bcum[C - 1:C, :]

    q_in = (qs * jnp.exp2(bcum)).astype(BF16)
    k_out = (kk * jnp.exp2(b_last - bcum)).astype(BF16)
    decay = jnp.exp2(b_last)
    o = []
    for h, hs in enumerate(heads):
        st = st_ref[h]
        o.append(lax.dot_general(q_in[:, hs], st.astype(BF16), nt, preferred_element_type=F32))
        upd = lax.dot_general(v[:, hs], k_out[:, hs], tn, preferred_element_type=F32)
        st_ref[h] = st * decay[:, hs] + upd

    q_sub = (qs * jnp.exp2(bsub)).astype(BF16)
    a = [jnp.zeros((C, C), F32) for _ in heads]
    for i in range(1, C // SUB):
        r_i = bcum[i * SUB - 1:i * SUB, :]
        k_i = (kk * jnp.exp2(jnp.minimum(r_i - bcum, 0.0))).astype(BF16)
        block = (row_sub == i) & (col < i * SUB)
        for h, hs in enumerate(heads):
            a_i = lax.dot_general(q_sub[:, hs], k_i[:, hs], nt, preferred_element_type=F32)
            a[h] = jnp.where(block, a_i, a[h])
    for h, hs in enumerate(heads):
        o[h] = o[h] + jnp.dot(a[h].astype(BF16), v[:, hs], preferred_element_type=F32)

    sub_pos = lax.broadcasted_iota(jnp.int32, (C, H * K), 0) % SUB
    ones = jnp.ones((K, LANES), BF16)
    vf = v.astype(F32)
    for d in range(SUB):
        if d == 0:
            e = qs * kk
            vd = vf
        else:
            e = qs * pltpu.roll(kk, d, 0) * jnp.exp2(bcum - pltpu.roll(bcum, d, 0))
            e = jnp.where(sub_pos >= d, e, 0.0)
            vd = pltpu.roll(vf, d, 0)
        e = e.astype(BF16)
        for h, hs in enumerate(heads):
            o[h] = o[h] + jnp.dot(e[:, hs], ones, preferred_element_type=F32) * vd[:, hs]
    o_ref[...] = jnp.concatenate(o, axis=1)


def hgrn_recurrence(q, f_logit, v, lb):
    T = q.shape[0]
    C = HGRN_CHUNK
    width = HGRN_HEADS * HGRN_EXPAND
    blk = pl.BlockSpec((C, width), lambda c: (c, 0))
    return pl.pallas_call(
        _hgrn_kernel,
        grid=(T // C,),
        in_specs=[blk, blk, blk, _const_spec((1, width))],
        out_specs=blk,
        out_shape=jax.ShapeDtypeStruct((T, width), F32),
        scratch_shapes=[pltpu.VMEM((HGRN_HEADS, HGRN_HEAD_DIM, HGRN_EXPAND), F32)],
        compiler_params=_cparams(("arbitrary",)),
        name="hgrn",
    )(q, f_logit, v, lb.reshape(1, -1))


FOX_CUM_TILE = 256


def _split3(x):
    hi = x.astype(BF16)
    r1 = x - hi.astype(F32)
    mid = r1.astype(BF16)
    lo = (r1 - mid.astype(F32)).astype(BF16)
    return hi, mid, lo


FOX_ONES_ROW = 3 * FOX_HEADS


def _fox_cumsum_kernel(f_ref, kaug_ref, ct_ref, carry_ref):
    @pl.when(pl.program_id(0) == 0)
    def _():
        carry_ref[...] = jnp.zeros_like(carry_ref)

    x = f_ref[...]
    ls = jnp.minimum(x, 0.0) - jnp.log(1.0 + jnp.exp(-jnp.abs(x)))
    n = FOX_CUM_TILE
    tril = (lax.broadcasted_iota(jnp.int32, (n, n), 0) >= lax.broadcasted_iota(jnp.int32, (n, n), 1)).astype(BF16)
    hi, mid, lo = _split3(ls)
    cum = (jnp.dot(tril, hi, preferred_element_type=F32)
           + jnp.dot(tril, mid, preferred_element_type=F32)
           + jnp.dot(tril, lo, preferred_element_type=F32))
    c = cum + carry_ref[...]
    carry_ref[...] = c[n - 1:n, :]
    H = FOX_HEADS
    hi2, mid2, lo2 = (a.astype(F32) for a in _split3(c * LOG2E))
    lane = lax.broadcasted_iota(jnp.int32, (n, LANES), 1)
    packed = jnp.where(lane < H, hi2,
                       jnp.where(lane < 2 * H, pltpu.roll(mid2, H, 1),
                                 jnp.where(lane < 3 * H, pltpu.roll(lo2, 2 * H, 1), 0.0)))
    is_one = (lane >= FOX_ONES_ROW) & (lane < FOX_ONES_ROW + 3)
    kaug_ref[...] = jnp.where(is_one, 1.0, -packed).astype(BF16)
    ct_ref[...] = packed.T


def fox_cumsum(f_logit):
    T = f_logit.shape[0]
    n = FOX_CUM_TILE
    spec = pl.BlockSpec((n, LANES), lambda i: (i, 0))
    return pl.pallas_call(
        _fox_cumsum_kernel,
        grid=(T // n,),
        in_specs=[spec],
        out_specs=[spec, pl.BlockSpec((LANES, n), lambda i: (0, i))],
        out_shape=[jax.ShapeDtypeStruct((T, LANES), BF16), jax.ShapeDtypeStruct((LANES, T), F32)],
        scratch_shapes=[pltpu.VMEM((1, LANES), F32)],
        compiler_params=_cparams(("arbitrary",)),
        name="fox_cumsum",
    )(f_logit)


FOX_TILE = 512
FOX_KEY_TILE = 256


FOX_VROWS = FOX_HEAD_DIM + 16


def _fox_kernel(k_ref, kaug_ref, qt_ref, ct_ref, vt_ref, o_ref, rhs_ref, acc_ref, m_ref, alpha_ref, st_ref, pt_ref):
    p = pl.program_id(0)
    i = pl.program_id(1)
    tq, tk = FOX_TILE, FOX_KEY_TILE
    dh = FOX_HEAD_DIM
    H = FOX_HEADS

    row = lax.broadcasted_iota(jnp.int32, (LANES, tq), 0)
    qt = qt_ref[...].astype(F32)
    for h in range(2):
        head = 2 * p + h
        q_part = jnp.where((row >= h * dh) & (row < (h + 1) * dh), qt, 0.0)
        gate_rows = (row == head) | (row == H + head) | (row == 2 * H + head)
        c_parts = [ct_ref[pl.ds(part * H + head, 1), :] for part in range(3)]
        aug = jnp.where(gate_rows, 1.0, 0.0)
        for part in range(3):
            aug = jnp.where(row == FOX_ONES_ROW + part, c_parts[part], aug)
        rhs_ref[h] = jnp.concatenate([q_part, aug], axis=0).astype(BF16)

    acc_ref[...] = jnp.zeros_like(acc_ref)
    m_ref[...] = jnp.full_like(m_ref, NEG_BIG)
    pt_ref[1] = jnp.zeros(pt_ref.shape[1:], BF16)
    alpha_ref[1] = jnp.ones(alpha_ref.shape[1:], F32)

    def logits(j, slot):
        start = pl.multiple_of(j * tk, tk)
        lhs = jnp.concatenate([k_ref[pl.ds(start, tk), :], kaug_ref[pl.ds(start, tk), :]], axis=1)
        for h in range(2):
            st_ref[slot, h] = jnp.dot(lhs, rhs_ref[h], preferred_element_type=F32)

    def softmax(slot, mask_offset=None):
        for h in range(2):
            st = st_ref[slot, h]
            if mask_offset is not None:
                causal = (lax.broadcasted_iota(jnp.int32, (tk, tq), 0) + mask_offset
                          <= lax.broadcasted_iota(jnp.int32, (tk, tq), 1))
                st = jnp.where(causal, st, NEG_BIG)
            m_old = m_ref[h]
            m_new = jnp.maximum(m_old, jnp.max(st, axis=0, keepdims=True))
            alpha_ref[slot, h] = jnp.exp2(m_old - m_new)
            pt_ref[slot, h] = jnp.exp2(st - m_new).astype(BF16)
            m_ref[h] = m_new

    ones_rows = jnp.ones((FOX_VROWS - dh, tk), BF16)

    def accum(j, slot):
        start = pl.multiple_of(j * tk, tk)
        for h in range(2):
            vt_aug = jnp.concatenate([vt_ref[h * dh:(h + 1) * dh, pl.ds(start, tk)], ones_rows], axis=0)
            pv = jnp.dot(vt_aug, pt_ref[slot, h], preferred_element_type=F32)
            acc_ref[h] = alpha_ref[slot, h] * acc_ref[h] + pv

    sub = tq // tk

    def body(p, carry):
        accum(jnp.maximum(sub * p - 1, 0), 1)
        softmax(0)
        logits(sub * p + 1, 1)
        accum(sub * p, 0)
        softmax(1)
        logits(sub * p + 2, 0)
        return carry

    logits(0, 0)
    lax.fori_loop(0, i, body, 0)
    accum(jnp.maximum(sub * i - 1, 0), 1)
    softmax(0, mask_offset=0)
    logits(sub * i + 1, 1)
    accum(sub * i, 0)
    softmax(1, mask_offset=tk)
    accum(sub * i + 1, 1)

    acc_a, acc_b = acc_ref[0], acc_ref[1]
    ot = jnp.concatenate([acc_a[:dh] / acc_a[dh:dh + 1], acc_b[:dh] / acc_b[dh:dh + 1]], axis=0)
    o_ref[...] = ot.T.astype(o_ref.dtype)


def fox_attention(k, kaug, qt, ct, vt):
    T = k.shape[0]
    t = FOX_TILE
    pairs = FOX_HEADS // 2
    return pl.pallas_call(
        _fox_kernel,
        grid=(pairs, T // t),
        in_specs=[
            pl.BlockSpec((T, LANES), lambda p, i: (0, p)),
            pl.BlockSpec((T, LANES), lambda p, i: (0, 0)),
            pl.BlockSpec((LANES, t), lambda p, i: (p, i)),
            pl.BlockSpec((LANES, t), lambda p, i: (0, i)),
            pl.BlockSpec((LANES, T), lambda p, i: (p, 0)),
        ],
        out_specs=pl.BlockSpec((t, LANES), lambda p, i: (i, p)),
        out_shape=jax.ShapeDtypeStruct((T, FOX_HEADS * FOX_HEAD_DIM), BF16),
        scratch_shapes=[
            pltpu.VMEM((2, 2 * LANES, t), BF16),
            pltpu.VMEM((2, FOX_VROWS, t), F32),
            pltpu.VMEM((2, 1, t), F32),
            pltpu.VMEM((2, 2, 1, t), F32),
            pltpu.VMEM((2, 2, FOX_KEY_TILE, t), F32),
            pltpu.VMEM((2, 2, FOX_KEY_TILE, t), BF16),
        ],
        compiler_params=_cparams(("parallel", "arbitrary")),
        name="fox",
    )(k, kaug, qt, ct, vt)


def kernel(x, norm_mix, norm_mlp, norm_final, w_up, w_down, swa_w_qkv, swa_b_qkv, swa_sinks, swa_w_o,
           hgrn_w_in, hgrn_lb_logits, hgrn_g_norm, hgrn_w_o, fox_w_in, fox_b_in, fox_w_o):
    B, T, D = x.shape
    depth = norm_mix.shape[0]
    lb_soft = jax.nn.softmax(hgrn_lb_logits.astype(F32), axis=0)
    lower_bounds = jnp.cumsum(lb_soft, axis=0) - lb_soft[0]

    outs = []
    for b in range(B):
        xb = x[b]
        for i in range(depth):
            m, j = i % N_MIXERS, i // N_MIXERS
            if m == 0:
                nq = SWA_HEADS * SWA_HEAD_DIM
                nkv = SWA_KV_HEADS * SWA_HEAD_DIM
                q, k, v = norm_proj(xb, norm_mix[i], swa_w_qkv[j].astype(BF16), swa_b_qkv[j],
                                    [(0, nq), (nq, nq + nkv), (nq + nkv, nq + 2 * nkv)], [BF16] * 3, "swa_proj")
                a = swa_attention(q, k, v, swa_sinks[j])
                xb = proj_residual(a, swa_w_o[j].astype(BF16), xb, "swa_out")
            elif m == 1:
                hk = HGRN_HEADS * HGRN_EXPAND
                hv = HGRN_HEADS * HGRN_HEAD_DIM
                q, f, v, g = norm_proj(xb, norm_mix[i], hgrn_w_in[j].astype(BF16), jnp.zeros((2 * hk + 2 * hv,), F32),
                                       [(0, hk), (hk, 2 * hk), (2 * hk, 2 * hk + hv), (2 * hk + hv, 2 * hk + 2 * hv)],
                                       [F32, F32, BF16, F32], "hgrn_proj")
                o = hgrn_recurrence(q, f, v, lower_bounds[i])
                xb = hgrn_out(o, g, hgrn_g_norm[j], hgrn_w_o[j].astype(BF16), xb)
            else:
                n = FOX_HEADS * FOX_HEAD_DIM
                pad = LANES - FOX_HEADS
                col_scale = jnp.concatenate([jnp.full((n,), LOG2E * FOX_HEAD_DIM ** -0.5, F32),
                                             jnp.ones((2 * n + FOX_HEADS,), F32)])
                w = jnp.pad(fox_w_in[j] * col_scale, ((0, 0), (0, pad))).astype(BF16)
                bias = jnp.pad(fox_b_in[j] * col_scale, ((0, pad),))
                qt, k, vt, f = norm_proj(xb, norm_mix[i], w, bias,
                                         [(0, n), (n, 2 * n), (2 * n, 3 * n), (3 * n, 3 * n + LANES)],
                                         [BF16, BF16, BF16, F32], "fox_proj",
                                         transposed=(True, False, True, False))
                kaug, ct = fox_cumsum(f)
                a = fox_attention(k, kaug, qt, ct, vt)
                xb = proj_residual(a, fox_w_o[j].astype(BF16), xb, "fox_out")
            last = i == depth - 1
            xb = mlp(xb, norm_mlp[i], w_up[i].astype(BF16), w_down[i].astype(BF16), norm_final, last)
        outs.append(xb)
    return jnp.stack(outs, axis=0)
```

```python
import functools

import jax
import jax.numpy as jnp
from jax import lax
from jax.experimental import pallas as pl
from jax.experimental.pallas import tpu as pltpu

F32 = jnp.float32
BF16 = jnp.bfloat16

D_MODEL = 1024
D_FF = 4 * D_MODEL
EPS = 1e-6
N_MIXERS = 3

SWA_HEADS = 16
SWA_KV_HEADS = 4
SWA_HEAD_DIM = 64
SWA_WINDOW = 128

HGRN_HEADS = 8
HGRN_EXPAND = 128
HGRN_HEAD_DIM = 128
HGRN_CHUNK = 64
HGRN_SUB = 8

FOX_HEADS = 16
FOX_HEAD_DIM = 64

LANES = 128
VMEM_LIMIT = 56 * 1024 * 1024
NEG_BIG = -1e30
LOG2E = 1.4426950408889634

ROW_TILE = 512


def _cparams(sem):
    return pltpu.CompilerParams(dimension_semantics=sem, vmem_limit_bytes=VMEM_LIMIT)


def _const_spec(shape):
    nd = len(shape)
    return pl.BlockSpec(shape, lambda *_: (0,) * nd, pipeline_mode=pl.Buffered(1))


def _rms(x, g):
    ms = jnp.mean(x * x, axis=-1, keepdims=True)
    return x * lax.rsqrt(ms + EPS) * g


def _norm_proj_kernel(x_ref, g_ref, w_ref, b_ref, *o_refs, segments, transposed, n_chunk):
    h = _rms(x_ref[...], g_ref[...]).astype(BF16)
    for o_ref, (lo, hi), tr in zip(o_refs, segments, transposed):
        for j in range(lo, hi, n_chunk):
            je = min(j + n_chunk, hi)
            y = jnp.dot(h, w_ref[:, j:je], preferred_element_type=F32) + b_ref[:, j:je]
            if tr:
                o_ref[j - lo:je - lo, :] = y.T.astype(o_ref.dtype)
            else:
                o_ref[:, j - lo:je - lo] = y.astype(o_ref.dtype)


def norm_proj(x, g, w, b, segments, dtypes, name, transposed=None):
    T, D = x.shape
    N = w.shape[1]
    tm = ROW_TILE
    transposed = tuple(transposed) if transposed is not None else (False,) * len(segments)
    out_shape, out_specs = [], []
    for (lo, hi), dt, tr in zip(segments, dtypes, transposed):
        if tr:
            out_shape.append(jax.ShapeDtypeStruct((hi - lo, T), dt))
            out_specs.append(pl.BlockSpec((hi - lo, tm), lambda i: (0, i)))
        else:
            out_shape.append(jax.ShapeDtypeStruct((T, hi - lo), dt))
            out_specs.append(pl.BlockSpec((tm, hi - lo), lambda i: (i, 0)))
    return pl.pallas_call(
        functools.partial(_norm_proj_kernel, segments=tuple(segments), transposed=transposed, n_chunk=512),
        grid=(T // tm,),
        in_specs=[
            pl.BlockSpec((tm, D), lambda i: (i, 0)),
            _const_spec((1, D)),
            _const_spec((D, N)),
            _const_spec((1, N)),
        ],
        out_specs=out_specs,
        out_shape=out_shape,
        compiler_params=_cparams(("parallel",)),
        name=name,
    )(x, g.reshape(1, D), w, b.reshape(1, N))


def _hgrn_gate_kernel(o_ref, g_ref, gn_ref, a_ref):
    for h in range(HGRN_HEADS):
        sl = slice(h * HGRN_HEAD_DIM, (h + 1) * HGRN_HEAD_DIM)
        o = o_ref[:, sl]
        o = o * lax.rsqrt(jnp.mean(o * o, axis=-1, keepdims=True) + EPS)
        g = g_ref[:, sl]
        a_ref[:, sl] = (o * gn_ref[:, sl] * (g * jax.nn.sigmoid(g))).astype(a_ref.dtype)


def hgrn_gate(o, g, g_norm):
    T, D = o.shape
    tm = ROW_TILE
    row = pl.BlockSpec((tm, D), lambda i: (i, 0))
    return pl.pallas_call(
        _hgrn_gate_kernel,
        grid=(T // tm,),
        in_specs=[row, row, _const_spec((1, D))],
        out_specs=row,
        out_shape=jax.ShapeDtypeStruct((T, D), BF16),
        compiler_params=_cparams(("parallel",)),
        name="hgrn_gate",
    )(o, g, g_norm.reshape(1, D))


def _out_mlp_kernel(a_ref, wo_ref, x_ref, g_ref, wu_ref, wd_ref, gf_ref, o_ref, *, ff_chunk, final_norm):
    x = x_ref[...] + jnp.dot(a_ref[...], wo_ref[...], preferred_element_type=F32)
    h = _rms(x, g_ref[...]).astype(BF16)
    acc = x
    for j in range(0, D_FF, ff_chunk):
        u = jnp.dot(h, wu_ref[:, j:j + ff_chunk], preferred_element_type=F32)
        u = jnp.maximum(u, 0.0)
        u = (u * u).astype(BF16)
        acc = acc + jnp.dot(u, wd_ref[j:j + ff_chunk, :], preferred_element_type=F32)
    if final_norm:
        acc = _rms(acc, gf_ref[...])
    o_ref[...] = acc


def out_mlp(a, w_o, x, g, w_up, w_down, g_final, final_norm):
    T, D = x.shape
    tm = ROW_TILE
    row = pl.BlockSpec((tm, D), lambda i: (i, 0))
    return pl.pallas_call(
        functools.partial(_out_mlp_kernel, ff_chunk=512, final_norm=final_norm),
        grid=(T // tm,),
        in_specs=[
            row,
            _const_spec((D, D)),
            row,
            _const_spec((1, D)),
            _const_spec((D, D_FF)),
            _const_spec((D_FF, D)),
            _const_spec((1, D)),
        ],
        out_specs=row,
        out_shape=jax.ShapeDtypeStruct((T, D), F32),
        compiler_params=_cparams(("parallel",)),
        name="out_mlp",
    )(a, w_o, x, g.reshape(1, D), w_up, w_down, g_final.reshape(1, D))


def _swa_kernel(sink_ref, q_ref, kc_ref, kp_ref, vc_ref, vp_ref, o_ref):
    n = pl.program_id(0)
    W, dh = SWA_WINDOW, SWA_HEAD_DIM
    G = SWA_HEADS // SWA_KV_HEADS
    qi = lax.broadcasted_iota(jnp.int32, (W, 2 * W), 0)
    kk = lax.broadcasted_iota(jnp.int32, (W, 2 * W), 1)
    rel = qi + W - kk
    visible = (rel >= 0) & (rel < W) & ((n > 0) | (kk >= W))
    outs = []
    for hk in range(SWA_KV_HEADS):
        ks = slice(hk * dh, (hk + 1) * dh)
        kb = jnp.concatenate([kp_ref[:, ks], kc_ref[:, ks]], axis=0)
        vb = jnp.concatenate([vp_ref[:, ks], vc_ref[:, ks]], axis=0)
        for g in range(G):
            head = hk * G + g
            q = q_ref[:, head * dh:(head + 1) * dh] * (dh ** -0.5)
            s = lax.dot_general(q, kb, (((1,), (1,)), ((), ())), preferred_element_type=F32)
            s = jnp.where(visible, s, NEG_BIG)
            sink = sink_ref[head]
            m = jnp.maximum(jnp.max(s, axis=-1, keepdims=True), sink)
            p = jnp.exp(s - m)
            denom = jnp.sum(p, axis=-1, keepdims=True) + jnp.exp(sink - m)
            o = jnp.dot(p.astype(BF16), vb, preferred_element_type=F32)
            outs.append(o / denom)
    o_ref[...] = jnp.concatenate(outs, axis=1).astype(o_ref.dtype)


def swa_attention(q, k, v, sinks):
    T = q.shape[0]
    W = SWA_WINDOW
    kvw = SWA_KV_HEADS * SWA_HEAD_DIM
    cur = lambda n: (n, 0)
    prev = lambda n: (jnp.maximum(n - 1, 0), 0)
    return pl.pallas_call(
        _swa_kernel,
        grid=(T // W,),
        in_specs=[
            pl.BlockSpec(memory_space=pltpu.SMEM),
            pl.BlockSpec((W, SWA_HEADS * SWA_HEAD_DIM), cur),
            pl.BlockSpec((W, kvw), cur),
            pl.BlockSpec((W, kvw), prev),
            pl.BlockSpec((W, kvw), cur),
            pl.BlockSpec((W, kvw), prev),
        ],
        out_specs=pl.BlockSpec((W, SWA_HEADS * SWA_HEAD_DIM), cur),
        out_shape=jax.ShapeDtypeStruct((T, SWA_HEADS * SWA_HEAD_DIM), BF16),
        compiler_params=_cparams(("parallel",)),
        name="swa",
    )(sinks, q, k, k, v, v)


def _split_dot(m_bf16, x):
    hi = x.astype(BF16)
    lo = (x - hi.astype(F32)).astype(BF16)
    return (jnp.dot(m_bf16, hi, preferred_element_type=F32)
            + jnp.dot(m_bf16, lo, preferred_element_type=F32))


def _hgrn_kernel(q_ref, f_ref, v_ref, lb_ref, o_ref, st_ref):
    C, SUB, K, H = HGRN_CHUNK, HGRN_SUB, HGRN_EXPAND, HGRN_HEADS
    nt = (((1,), (1,)), ((), ()))
    tn = (((0,), (0,)), ((), ()))

    @pl.when(pl.program_id(0) == 0)
    def _():
        st_ref[...] = jnp.zeros_like(st_ref)

    row = lax.broadcasted_iota(jnp.int32, (C, C), 0)
    col = lax.broadcasted_iota(jnp.int32, (C, C), 1)
    tril = row >= col
    row_sub = row // SUB
    heads = [slice(h * K, (h + 1) * K) for h in range(H)]

    lb = lb_ref[...]
    f = lb + (1.0 - lb) * jax.nn.sigmoid(f_ref[...])
    logf = jnp.log2(f)
    kk = 1.0 - f
    q = q_ref[...]
    qs = q * jax.nn.sigmoid(q)
    v = v_ref[...]
    bcum = _split_dot(tril.astype(BF16), logf)
    bsub = _split_dot((tril & (row_sub == col // SUB)).astype(BF16), logf)
    b_last = bcum[C - 1:C, :]

    q_in = (qs * jnp.exp2(bcum)).astype(BF16)
    k_out = (kk * jnp.exp2(b_last - bcum)).astype(BF16)
    decay = jnp.exp2(b_last)
    o = []
    for h, hs in enumerate(heads):
        st = st_ref[h]
        o.append(lax.dot_general(q_in[:, hs], st.astype(BF16), nt, preferred_element_type=F32))
        upd = lax.dot_general(v[:, hs], k_out[:, hs], tn, preferred_element_type=F32)
        st_ref[h] = st * decay[:, hs] + upd

    q_sub = (qs * jnp.exp2(bsub)).astype(BF16)
    a = [jnp.zeros((C, C), F32) for _ in heads]
    for i in range(1, C // SUB):
        r_i = bcum[i * SUB - 1:i * SUB, :]
        k_i = (kk * jnp.exp2(jnp.minimum(r_i - bcum, 0.0))).astype(BF16)
        block = (row_sub == i) & (col < i * SUB)
        for h, hs in enumerate(heads):
            a_i = lax.dot_general(q_sub[:, hs], k_i[:, hs], nt, preferred_element_type=F32)
            a[h] = jnp.where(block, a_i, a[h])
    for h, hs in enumerate(heads):
        o[h] = o[h] + jnp.dot(a[h].astype(BF16), v[:, hs], preferred_element_type=F32)

    sub_pos = lax.broadcasted_iota(jnp.int32, (C, H * K), 0) % SUB
    ones = jnp.ones((K, LANES), BF16)
    vf = v.astype(F32)
    for d in range(SUB):
        if d == 0:
            e = qs * kk
            vd = vf
        else:
            e = qs * pltpu.roll(kk, d, 0) * jnp.exp2(bcum - pltpu.roll(bcum, d, 0))
            e = jnp.where(sub_pos >= d, e, 0.0)
            vd = pltpu.roll(vf, d, 0)
        e = e.astype(BF16)
        for h, hs in enumerate(heads):
            o[h] = o[h] + jnp.dot(e[:, hs], ones, preferred_element_type=F32) * vd[:, hs]
    o_ref[...] = jnp.concatenate(o, axis=1)


def hgrn_recurrence(q, f_logit, v, lb):
    T = q.shape[0]
    C = HGRN_CHUNK
    width = HGRN_HEADS * HGRN_EXPAND
    blk = pl.BlockSpec((C, width), lambda c: (c, 0))
    return pl.pallas_call(
        _hgrn_kernel,
        grid=(T // C,),
        in_specs=[blk, blk, blk, _const_spec((1, width))],
        out_specs=blk,
        out_shape=jax.ShapeDtypeStruct((T, width), F32),
        scratch_shapes=[pltpu.VMEM((HGRN_HEADS, HGRN_HEAD_DIM, HGRN_EXPAND), F32)],
        compiler_params=_cparams(("arbitrary",)),
        name="hgrn",
    )(q, f_logit, v, lb.reshape(1, -1))


FOX_CUM_TILE = 256


def _split3(x):
    hi = x.astype(BF16)
    r1 = x - hi.astype(F32)
    mid = r1.astype(BF16)
    lo = (r1 - mid.astype(F32)).astype(BF16)
    return hi, mid, lo


FOX_ONES_ROW = 3 * FOX_HEADS


def _fox_cumsum_kernel(f_ref, kaug_ref, ct_ref, carry_ref):
    @pl.when(pl.program_id(0) == 0)
    def _():
        carry_ref[...] = jnp.zeros_like(carry_ref)

    x = f_ref[...]
    ls = jnp.minimum(x, 0.0) - jnp.log(1.0 + jnp.exp(-jnp.abs(x)))
    n = FOX_CUM_TILE
    tril = (lax.broadcasted_iota(jnp.int32, (n, n), 0) >= lax.broadcasted_iota(jnp.int32, (n, n), 1)).astype(BF16)
    hi, mid, lo = _split3(ls)
    cum = (jnp.dot(tril, hi, preferred_element_type=F32)
           + jnp.dot(tril, mid, preferred_element_type=F32)
           + jnp.dot(tril, lo, preferred_element_type=F32))
    c = cum + carry_ref[...]
    carry_ref[...] = c[n - 1:n, :]
    H = FOX_HEADS
    hi2, mid2, lo2 = (a.astype(F32) for a in _split3(c * LOG2E))
    lane = lax.broadcasted_iota(jnp.int32, (n, LANES), 1)
    packed = jnp.where(lane < H, hi2,
                       jnp.where(lane < 2 * H, pltpu.roll(mid2, H, 1),
                                 jnp.where(lane < 3 * H, pltpu.roll(lo2, 2 * H, 1), 0.0)))
    is_one = (lane >= FOX_ONES_ROW) & (lane < FOX_ONES_ROW + 3)
    kaug_ref[...] = jnp.where(is_one, 1.0, -packed).astype(BF16)
    ct_ref[...] = packed.T


def fox_cumsum(f_logit):
    T = f_logit.shape[0]
    n = FOX_CUM_TILE
    spec = pl.BlockSpec((n, LANES), lambda i: (i, 0))
    return pl.pallas_call(
        _fox_cumsum_kernel,
        grid=(T // n,),
        in_specs=[spec],
        out_specs=[spec, pl.BlockSpec((LANES, n), lambda i: (0, i))],
        out_shape=[jax.ShapeDtypeStruct((T, LANES), BF16), jax.ShapeDtypeStruct((LANES, T), F32)],
        scratch_shapes=[pltpu.VMEM((1, LANES), F32)],
        compiler_params=_cparams(("arbitrary",)),
        name="fox_cumsum",
    )(f_logit)


FOX_TILE = 512
FOX_KEY_TILE = 256


FOX_VROWS = FOX_HEAD_DIM + 16


def _fox_kernel(first_ref, k_ref, kaug_ref, qt_ref, ct_ref, vt_ref, o_ref,
                rhs_ref, acc_ref, m_ref, alpha_ref, st_ref, pt_ref):
    p = pl.program_id(0)
    i = pl.program_id(1)
    tq, tk = FOX_TILE, FOX_KEY_TILE
    dh = FOX_HEAD_DIM
    H = FOX_HEADS

    row = lax.broadcasted_iota(jnp.int32, (LANES, tq), 0)
    qt = qt_ref[...].astype(F32)
    for h in range(2):
        head = 2 * p + h
        q_part = jnp.where((row >= h * dh) & (row < (h + 1) * dh), qt, 0.0)
        gate_rows = (row == head) | (row == H + head) | (row == 2 * H + head)
        c_parts = [ct_ref[pl.ds(part * H + head, 1), :] for part in range(3)]
        aug = jnp.where(gate_rows, 1.0, 0.0)
        for part in range(3):
            aug = jnp.where(row == FOX_ONES_ROW + part, c_parts[part], aug)
        rhs_ref[h] = jnp.concatenate([q_part, aug], axis=0).astype(BF16)

    acc_ref[...] = jnp.zeros_like(acc_ref)
    m_ref[...] = jnp.full_like(m_ref, NEG_BIG)
    pt_ref[1] = jnp.zeros(pt_ref.shape[1:], BF16)
    alpha_ref[1] = jnp.ones(alpha_ref.shape[1:], F32)

    def logits(j, slot):
        start = pl.multiple_of(j * tk, tk)
        lhs = jnp.concatenate([k_ref[pl.ds(start, tk), :], kaug_ref[pl.ds(start, tk), :]], axis=1)
        for h in range(2):
            st_ref[slot, h] = jnp.dot(lhs, rhs_ref[h], preferred_element_type=F32)

    def softmax(slot, mask_offset=None):
        for h in range(2):
            st = st_ref[slot, h]
            if mask_offset is not None:
                causal = (lax.broadcasted_iota(jnp.int32, (tk, tq), 0) + mask_offset
                          <= lax.broadcasted_iota(jnp.int32, (tk, tq), 1))
                st = jnp.where(causal, st, NEG_BIG)
            m_old = m_ref[h]
            m_new = jnp.maximum(m_old, jnp.max(st, axis=0, keepdims=True))
            alpha_ref[slot, h] = jnp.exp2(m_old - m_new)
            pt_ref[slot, h] = jnp.exp2(st - m_new).astype(BF16)
            m_ref[h] = m_new

    ones_rows = jnp.ones((FOX_VROWS - dh, tk), BF16)

    def accum(j, slot):
        start = pl.multiple_of(j * tk, tk)
        for h in range(2):
            vt_aug = jnp.concatenate([vt_ref[h * dh:(h + 1) * dh, pl.ds(start, tk)], ones_rows], axis=0)
            pv = jnp.dot(vt_aug, pt_ref[slot, h], preferred_element_type=F32)
            acc_ref[h] = alpha_ref[slot, h] * acc_ref[h] + pv

    sub = tq // tk

    def body(p, carry):
        accum(jnp.maximum(sub * p - 1, 0), 1)
        softmax(0)
        logits(sub * p + 1, 1)
        accum(sub * p, 0)
        softmax(1)
        logits(sub * p + 2, 0)
        return carry

    first = first_ref[p, i]
    logits(sub * first, 0)
    lax.fori_loop(first, i, body, 0)
    accum(jnp.maximum(sub * i - 1, 0), 1)
    softmax(0, mask_offset=0)
    logits(sub * i + 1, 1)
    accum(sub * i, 0)
    softmax(1, mask_offset=tk)
    accum(sub * i + 1, 1)

    acc_a, acc_b = acc_ref[0], acc_ref[1]
    ot = jnp.concatenate([acc_a[:dh] / acc_a[dh:dh + 1], acc_b[:dh] / acc_b[dh:dh + 1]], axis=0)
    o_ref[...] = ot.T.astype(o_ref.dtype)


def _fox_norms_kernel(k_ref, qt_ref, kn_ref, qn_ref):
    n = FOX_HEADS * FOX_HEAD_DIM
    chan = lax.broadcasted_iota(jnp.int32, (n, LANES), 0) // FOX_HEAD_DIM
    head = lax.broadcasted_iota(jnp.int32, (n, LANES), 1)
    group = (chan == head).astype(BF16)
    k = k_ref[...].astype(F32)
    k2 = jnp.dot((k * k).astype(BF16), group, preferred_element_type=F32)
    q = qt_ref[...].astype(F32)
    q2 = lax.dot_general((q * q).astype(BF16), group, (((0,), (0,)), ((), ())), preferred_element_type=F32)
    kn_ref[0] = jnp.broadcast_to(jnp.max(k2, axis=0, keepdims=True), kn_ref.shape[1:])
    qn_ref[0] = jnp.broadcast_to(jnp.max(q2, axis=0, keepdims=True), qn_ref.shape[1:])


def fox_norms(k, qt):
    T = k.shape[0]
    t = FOX_TILE
    n = FOX_HEADS * FOX_HEAD_DIM
    out = pl.BlockSpec((1, 8, LANES), lambda i: (i, 0, 0))
    return pl.pallas_call(
        _fox_norms_kernel,
        grid=(T // t,),
        in_specs=[pl.BlockSpec((t, n), lambda i: (i, 0)), pl.BlockSpec((n, t), lambda i: (0, i))],
        out_specs=[out, out],
        out_shape=[jax.ShapeDtypeStruct((T // t, 8, LANES), F32)] * 2,
        compiler_params=_cparams(("parallel",)),
        name="fox_norms",
    )(k, qt)


FOX_UNDERFLOW_LOG2 = 160.0


def fox_first_block(kn2, qn2, ct):
    H = FOX_HEADS
    nt = kn2.shape[0]
    bk = jnp.sqrt(jnp.max(kn2[:, 0, :H], axis=0))
    bq = jnp.sqrt(qn2[:, 0, :H])
    need = 2.0 * 1.02 * bq * bk[None, :] + FOX_UNDERFLOW_LOG2 + 2.0
    c_first = ct[:, 0::FOX_TILE]
    c_last = ct[:, FOX_TILE - 1::FOX_TILE]
    c_first = c_first[:H] + c_first[H:2 * H] + c_first[2 * H:3 * H]
    c_last = c_last[:H] + c_last[H:2 * H] + c_last[2 * H:3 * H]
    gap = c_last[:, None, :] - c_first[:, :, None]
    earlier = jnp.arange(nt)[None, :] < jnp.arange(nt)[:, None]
    prunable = (gap > need.T[:, :, None]) & earlier[None]
    n_skip = jnp.min(jnp.where(prunable, nt, jnp.arange(nt)[None, None, :]), axis=2)
    return jnp.min(n_skip.reshape(H // 2, 2, nt), axis=1).astype(jnp.int32)


def fox_attention(first, k, kaug, qt, ct, vt):
    T = k.shape[0]
    t = FOX_TILE
    pairs = FOX_HEADS // 2
    return pl.pallas_call(
        _fox_kernel,
        grid=(pairs, T // t),
        in_specs=[
            pl.BlockSpec(memory_space=pltpu.SMEM),
            pl.BlockSpec((T, LANES), lambda p, i: (0, p)),
            pl.BlockSpec((T, LANES), lambda p, i: (0, 0)),
            pl.BlockSpec((LANES, t), lambda p, i: (p, i)),
            pl.BlockSpec((LANES, t), lambda p, i: (0, i)),
            pl.BlockSpec((LANES, T), lambda p, i: (p, 0)),
        ],
        out_specs=pl.BlockSpec((t, LANES), lambda p, i: (i, p)),
        out_shape=jax.ShapeDtypeStruct((T, FOX_HEADS * FOX_HEAD_DIM), BF16),
        scratch_shapes=[
            pltpu.VMEM((2, 2 * LANES, t), BF16),
            pltpu.VMEM((2, FOX_VROWS, t), F32),
            pltpu.VMEM((2, 1, t), F32),
            pltpu.VMEM((2, 2, 1, t), F32),
            pltpu.VMEM((2, 2, FOX_KEY_TILE, t), F32),
            pltpu.VMEM((2, 2, FOX_KEY_TILE, t), BF16),
        ],
        compiler_params=_cparams(("parallel", "arbitrary")),
        name="fox",
    )(first, k, kaug, qt, ct, vt)


def kernel(x, norm_mix, norm_mlp, norm_final, w_up, w_down, swa_w_qkv, swa_b_qkv, swa_sinks, swa_w_o,
           hgrn_w_in, hgrn_lb_logits, hgrn_g_norm, hgrn_w_o, fox_w_in, fox_b_in, fox_w_o):
    B, T, D = x.shape
    depth = norm_mix.shape[0]
    lb_soft = jax.nn.softmax(hgrn_lb_logits.astype(F32), axis=0)
    lower_bounds = jnp.cumsum(lb_soft, axis=0) - lb_soft[0]

    outs = []
    for b in range(B):
        xb = x[b]
        for i in range(depth):
            m, j = i % N_MIXERS, i // N_MIXERS
            if m == 0:
                nq = SWA_HEADS * SWA_HEAD_DIM
                nkv = SWA_KV_HEADS * SWA_HEAD_DIM
                q, k, v = norm_proj(xb, norm_mix[i], swa_w_qkv[j].astype(BF16), swa_b_qkv[j],
                                    [(0, nq), (nq, nq + nkv), (nq + nkv, nq + 2 * nkv)], [BF16] * 3, "swa_proj")
                a = swa_attention(q, k, v, swa_sinks[j])
                w_o = swa_w_o[j]
            elif m == 1:
                hk = HGRN_HEADS * HGRN_EXPAND
                hv = HGRN_HEADS * HGRN_HEAD_DIM
                q, f, v, g = norm_proj(xb, norm_mix[i], hgrn_w_in[j].astype(BF16), jnp.zeros((2 * hk + 2 * hv,), F32),
                                       [(0, hk), (hk, 2 * hk), (2 * hk, 2 * hk + hv), (2 * hk + hv, 2 * hk + 2 * hv)],
                                       [F32, F32, BF16, F32], "hgrn_proj")
                o = hgrn_recurrence(q, f, v, lower_bounds[i])
                a = hgrn_gate(o, g, hgrn_g_norm[j])
                w_o = hgrn_w_o[j]
            else:
                n = FOX_HEADS * FOX_HEAD_DIM
                pad = LANES - FOX_HEADS
                col_scale = jnp.concatenate([jnp.full((n,), LOG2E * FOX_HEAD_DIM ** -0.5, F32),
                                             jnp.ones((2 * n + FOX_HEADS,), F32)])
                w = jnp.pad(fox_w_in[j] * col_scale, ((0, 0), (0, pad))).astype(BF16)
                bias = jnp.pad(fox_b_in[j] * col_scale, ((0, pad),))
                qt, k, vt, f = norm_proj(xb, norm_mix[i], w, bias,
                                         [(0, n), (n, 2 * n), (2 * n, 3 * n), (3 * n, 3 * n + LANES)],
                                         [BF16, BF16, BF16, F32], "fox_proj",
                                         transposed=(True, False, True, False))
                kaug, ct = fox_cumsum(f)
                first = fox_first_block(*fox_norms(k, qt), ct)
                a = fox_attention(first, k, kaug, qt, ct, vt)
                w_o = fox_w_o[j]
            last = i == depth - 1
            xb = out_mlp(a, w_o.astype(BF16), xb, norm_mlp[i], w_up[i].astype(BF16), w_down[i].astype(BF16),
                         norm_final, last)
        outs.append(xb)
    return jnp.stack(outs, axis=0)
```

```python
import functools

import jax
import jax.numpy as jnp
from jax import lax
from jax.experimental import pallas as pl
from jax.experimental.pallas import tpu as pltpu

F32 = jnp.float32
BF16 = jnp.bfloat16

D_MODEL = 1024
D_FF = 4 * D_MODEL
EPS = 1e-6
N_MIXERS = 3

SWA_HEADS = 16
SWA_KV_HEADS = 4
SWA_HEAD_DIM = 64
SWA_WINDOW = 128

HGRN_HEADS = 8
HGRN_EXPAND = 128
HGRN_HEAD_DIM = 128
HGRN_CHUNK = 64
HGRN_SUB = 8

FOX_HEADS = 16
FOX_HEAD_DIM = 64

LANES = 128
VMEM_LIMIT = 56 * 1024 * 1024
NEG_BIG = -1e30
LOG2E = 1.4426950408889634

ROW_TILE = 1024


def _cparams(sem):
    return pltpu.CompilerParams(dimension_semantics=sem, vmem_limit_bytes=VMEM_LIMIT)


def _const_spec(shape):
    nd = len(shape)
    return pl.BlockSpec(shape, lambda *_: (0,) * nd, pipeline_mode=pl.Buffered(1))


def _rms(x, g):
    ms = jnp.mean(x * x, axis=-1, keepdims=True)
    return x * lax.rsqrt(ms + EPS) * g


def _norm_proj_kernel(x_ref, g_ref, w_ref, b_ref, *o_refs, segments, transposed, n_chunk):
    h = _rms(x_ref[...], g_ref[...]).astype(BF16)
    for o_ref, (lo, hi), tr in zip(o_refs, segments, transposed):
        for j in range(lo, hi, n_chunk):
            je = min(j + n_chunk, hi)
            y = jnp.dot(h, w_ref[:, j:je], preferred_element_type=F32) + b_ref[:, j:je]
            if tr:
                o_ref[j - lo:je - lo, :] = y.T.astype(o_ref.dtype)
            else:
                o_ref[:, j - lo:je - lo] = y.astype(o_ref.dtype)


def norm_proj(x, g, w, b, segments, dtypes, name, transposed=None):
    T, D = x.shape
    N = w.shape[1]
    tm = ROW_TILE
    transposed = tuple(transposed) if transposed is not None else (False,) * len(segments)
    out_shape, out_specs = [], []
    for (lo, hi), dt, tr in zip(segments, dtypes, transposed):
        if tr:
            out_shape.append(jax.ShapeDtypeStruct((hi - lo, T), dt))
            out_specs.append(pl.BlockSpec((hi - lo, tm), lambda i: (0, i)))
        else:
            out_shape.append(jax.ShapeDtypeStruct((T, hi - lo), dt))
            out_specs.append(pl.BlockSpec((tm, hi - lo), lambda i: (i, 0)))
    return pl.pallas_call(
        functools.partial(_norm_proj_kernel, segments=tuple(segments), transposed=transposed, n_chunk=512),
        grid=(T // tm,),
        in_specs=[
            pl.BlockSpec((tm, D), lambda i: (i, 0)),
            _const_spec((1, D)),
            _const_spec((D, N)),
            _const_spec((1, N)),
        ],
        out_specs=out_specs,
        out_shape=out_shape,
        compiler_params=_cparams(("parallel",)),
        name=name,
    )(x, g.reshape(1, D), w, b.reshape(1, N))


def _hgrn_gate_kernel(o_ref, g_ref, gn_ref, a_ref):
    for h in range(HGRN_HEADS):
        sl = slice(h * HGRN_HEAD_DIM, (h + 1) * HGRN_HEAD_DIM)
        o = o_ref[:, sl]
        o = o * lax.rsqrt(jnp.mean(o * o, axis=-1, keepdims=True) + EPS)
        g = g_ref[:, sl]
        a_ref[:, sl] = (o * gn_ref[:, sl] * (g * jax.nn.sigmoid(g))).astype(a_ref.dtype)


def hgrn_gate(o, g, g_norm):
    T, D = o.shape
    tm = ROW_TILE
    row = pl.BlockSpec((tm, D), lambda i: (i, 0))
    return pl.pallas_call(
        _hgrn_gate_kernel,
        grid=(T // tm,),
        in_specs=[row, row, _const_spec((1, D))],
        out_specs=row,
        out_shape=jax.ShapeDtypeStruct((T, D), BF16),
        compiler_params=_cparams(("parallel",)),
        name="hgrn_gate",
    )(o, g, g_norm.reshape(1, D))


def _out_mlp_kernel(a_ref, wo_ref, x_ref, g_ref, wu_ref, wd_ref, gf_ref, o_ref, *, ff_chunk, final_norm):
    x = x_ref[...] + jnp.dot(a_ref[...], wo_ref[...], preferred_element_type=F32)
    h = _rms(x, g_ref[...]).astype(BF16)
    acc = x
    for j in range(0, D_FF, ff_chunk):
        u = jnp.dot(h, wu_ref[:, j:j + ff_chunk], preferred_element_type=F32)
        u = jnp.maximum(u, 0.0)
        u = (u * u).astype(BF16)
        acc = acc + jnp.dot(u, wd_ref[j:j + ff_chunk, :], preferred_element_type=F32)
    if final_norm:
        acc = _rms(acc, gf_ref[...])
    o_ref[...] = acc


def out_mlp(a, w_o, x, g, w_up, w_down, g_final, final_norm):
    T, D = x.shape
    tm = ROW_TILE
    row = pl.BlockSpec((tm, D), lambda i: (i, 0))
    return pl.pallas_call(
        functools.partial(_out_mlp_kernel, ff_chunk=512, final_norm=final_norm),
        grid=(T // tm,),
        in_specs=[
            row,
            _const_spec((D, D)),
            row,
            _const_spec((1, D)),
            _const_spec((D, D_FF)),
            _const_spec((D_FF, D)),
            _const_spec((1, D)),
        ],
        out_specs=row,
        out_shape=jax.ShapeDtypeStruct((T, D), F32),
        compiler_params=_cparams(("parallel",)),
        name="out_mlp",
    )(a, w_o, x, g.reshape(1, D), w_up, w_down, g_final.reshape(1, D))


SWA_VROWS = SWA_HEAD_DIM + 16


def _swa_kernel(sink_ref, qt_ref, kc_ref, kp_ref, vtc_ref, vtp_ref, o_ref):
    n = pl.program_id(0)
    W, dh = SWA_WINDOW, SWA_HEAD_DIM
    G = SWA_HEADS // SWA_KV_HEADS
    key = lax.broadcasted_iota(jnp.int32, (2 * W, G * W), 0)
    lane = lax.broadcasted_iota(jnp.int32, (2 * W, G * W), 1)
    rel = lane % W + W - key
    visible = (rel >= 0) & (rel < W) & ((n > 0) | (key >= W))
    group_lane = lax.broadcasted_iota(jnp.int32, (1, G * W), 1) // W
    k_band = jnp.concatenate([kp_ref[...], kc_ref[...]], axis=0)
    vt_band = jnp.concatenate([vtp_ref[...], vtc_ref[...]], axis=1)
    ones_rows = jnp.ones((SWA_VROWS - dh, 2 * W), BF16)
    kv_heads = range(SWA_KV_HEADS)

    sts = []
    for hk in kv_heads:
        pair, half = divmod(hk, 2)
        lhs = k_band[:, pair * LANES:(pair + 1) * LANES]
        q_cat = jnp.concatenate([qt_ref[(hk * G + g) * dh:(hk * G + g + 1) * dh, :] for g in range(G)], axis=1)
        zeros = jnp.zeros_like(q_cat)
        rhs = jnp.concatenate([q_cat, zeros] if half == 0 else [zeros, q_cat], axis=0)
        sts.append(jnp.dot(lhs, rhs, preferred_element_type=F32))

    pts, sink_terms = [], []
    for hk in kv_heads:
        st = jnp.where(visible, sts[hk], NEG_BIG)
        sink = jnp.zeros((1, G * W), F32)
        for g in range(G):
            sink = jnp.where(group_lane == g, sink_ref[hk * G + g] * LOG2E, sink)
        m = jnp.maximum(jnp.max(st, axis=0, keepdims=True), sink)
        pts.append(jnp.exp2(st - m).astype(BF16))
        sink_terms.append(jnp.exp2(sink - m))

    accs = []
    for hk in kv_heads:
        vt_aug = jnp.concatenate([vt_band[hk * dh:(hk + 1) * dh, :], ones_rows], axis=0)
        accs.append(jnp.dot(vt_aug, pts[hk], preferred_element_type=F32))

    outs = []
    for hk in kv_heads:
        acc = accs[hk]
        ot = acc[:dh] / (acc[dh:dh + 1] + sink_terms[hk])
        for g in range(0, G, 2):
            two = jnp.concatenate([ot[:, g * W:(g + 1) * W], ot[:, (g + 1) * W:(g + 2) * W]], axis=0)
            outs.append(two.T)
    o_ref[...] = jnp.concatenate(outs, axis=1).astype(o_ref.dtype)


def swa_attention(qt, k, vt, sinks):
    T = k.shape[0]
    W = SWA_WINDOW
    nq = SWA_HEADS * SWA_HEAD_DIM
    kvw = SWA_KV_HEADS * SWA_HEAD_DIM
    return pl.pallas_call(
        _swa_kernel,
        grid=(T // W,),
        in_specs=[
            pl.BlockSpec(memory_space=pltpu.SMEM),
            pl.BlockSpec((nq, W), lambda n: (0, n)),
            pl.BlockSpec((W, kvw), lambda n: (n, 0)),
            pl.BlockSpec((W, kvw), lambda n: (jnp.maximum(n - 1, 0), 0)),
            pl.BlockSpec((kvw, W), lambda n: (0, n)),
            pl.BlockSpec((kvw, W), lambda n: (0, jnp.maximum(n - 1, 0))),
        ],
        out_specs=pl.BlockSpec((W, nq), lambda n: (n, 0)),
        out_shape=jax.ShapeDtypeStruct((T, nq), BF16),
        compiler_params=_cparams(("parallel",)),
        name="swa",
    )(sinks, qt, k, k, vt, vt)


def _split_dot(m_bf16, x):
    hi = x.astype(BF16)
    lo = (x - hi.astype(F32)).astype(BF16)
    return (jnp.dot(m_bf16, hi, preferred_element_type=F32)
            + jnp.dot(m_bf16, lo, preferred_element_type=F32))


def _hgrn_kernel(q_ref, f_ref, v_ref, lb_ref, o_ref, st_ref):
    C, SUB, K, H = HGRN_CHUNK, HGRN_SUB, HGRN_EXPAND, HGRN_HEADS
    nt = (((1,), (1,)), ((), ()))
    tn = (((0,), (0,)), ((), ()))

    @pl.when(pl.program_id(0) == 0)
    def _():
        st_ref[...] = jnp.zeros_like(st_ref)

    row = lax.broadcasted_iota(jnp.int32, (C, C), 0)
    col = lax.broadcasted_iota(jnp.int32, (C, C), 1)
    tril = row >= col
    row_sub = row // SUB
    heads = [slice(h * K, (h + 1) * K) for h in range(H)]

    lb = lb_ref[...]
    f = lb + (1.0 - lb) * jax.nn.sigmoid(f_ref[...])
    logf = jnp.log2(f)
    kk = 1.0 - f
    q = q_ref[...]
    qs = q * jax.nn.sigmoid(q)
    v = v_ref[...]
    bcum = _split_dot(tril.astype(BF16), logf)
    bsub = _split_dot((tril & (row_sub == col // SUB)).astype(BF16), logf)
    b_last = bcum[C - 1:C, :]

    q_in = (qs * jnp.exp2(bcum)).astype(BF16)
    k_out = (kk * jnp.exp2(b_last - bcum)).astype(BF16)
    decay = jnp.exp2(b_last)
    o = []
    for h, hs in enumerate(heads):
        st = st_ref[h]
        o.append(lax.dot_general(q_in[:, hs], st.astype(BF16), nt, preferred_element_type=F32))
        upd = lax.dot_general(v[:, hs], k_out[:, hs], tn, preferred_element_type=F32)
        st_ref[h] = st * decay[:, hs] + upd

    q_sub = (qs * jnp.exp2(bsub)).astype(BF16)
    a = [jnp.zeros((C, C), F32) for _ in heads]
    for i in range(1, C // SUB):
        r_i = bcum[i * SUB - 1:i * SUB, :]
        k_i = (kk * jnp.exp2(jnp.minimum(r_i - bcum, 0.0))).astype(BF16)
        block = (row_sub == i) & (col < i * SUB)
        for h, hs in enumerate(heads):
            a_i = lax.dot_general(q_sub[:, hs], k_i[:, hs], nt, preferred_element_type=F32)
            a[h] = jnp.where(block, a_i, a[h])
    for h, hs in enumerate(heads):
        o[h] = o[h] + jnp.dot(a[h].astype(BF16), v[:, hs], preferred_element_type=F32)

    sub_pos = lax.broadcasted_iota(jnp.int32, (C, H * K), 0) % SUB
    ones = jnp.ones((K, LANES), BF16)
    vf = v.astype(F32)
    for d in range(SUB):
        if d == 0:
            e = qs * kk
            vd = vf
        else:
            e = qs * pltpu.roll(kk, d, 0) * jnp.exp2(bcum - pltpu.roll(bcum, d, 0))
            e = jnp.where(sub_pos >= d, e, 0.0)
            vd = pltpu.roll(vf, d, 0)
        e = e.astype(BF16)
        for h, hs in enumerate(heads):
            o[h] = o[h] + jnp.dot(e[:, hs], ones, preferred_element_type=F32) * vd[:, hs]
    o_ref[...] = jnp.concatenate(o, axis=1)


def hgrn_recurrence(q, f_logit, v, lb):
    T = q.shape[0]
    C = HGRN_CHUNK
    width = HGRN_HEADS * HGRN_EXPAND
    blk = pl.BlockSpec((C, width), lambda c: (c, 0))
    return pl.pallas_call(
        _hgrn_kernel,
        grid=(T // C,),
        in_specs=[blk, blk, blk, _const_spec((1, width))],
        out_specs=blk,
        out_shape=jax.ShapeDtypeStruct((T, width), F32),
        scratch_shapes=[pltpu.VMEM((HGRN_HEADS, HGRN_HEAD_DIM, HGRN_EXPAND), F32)],
        compiler_params=_cparams(("arbitrary",)),
        name="hgrn",
    )(q, f_logit, v, lb.reshape(1, -1))


FOX_CUM_TILE = 256


def _split3(x):
    hi = x.astype(BF16)
    r1 = x - hi.astype(F32)
    mid = r1.astype(BF16)
    lo = (r1 - mid.astype(F32)).astype(BF16)
    return hi, mid, lo


FOX_ONES_ROW = 3 * FOX_HEADS


def _fox_cumsum_kernel(f_ref, kaug_ref, ct_ref, carry_ref):
    @pl.when(pl.program_id(0) == 0)
    def _():
        carry_ref[...] = jnp.zeros_like(carry_ref)

    x = f_ref[...]
    ls = jnp.minimum(x, 0.0) - jnp.log(1.0 + jnp.exp(-jnp.abs(x)))
    n = FOX_CUM_TILE
    tril = (lax.broadcasted_iota(jnp.int32, (n, n), 0) >= lax.broadcasted_iota(jnp.int32, (n, n), 1)).astype(BF16)
    hi, mid, lo = _split3(ls)
    cum = (jnp.dot(tril, hi, preferred_element_type=F32)
           + jnp.dot(tril, mid, preferred_element_type=F32)
           + jnp.dot(tril, lo, preferred_element_type=F32))
    c = cum + carry_ref[...]
    carry_ref[...] = c[n - 1:n, :]
    H = FOX_HEADS
    hi2, mid2, lo2 = (a.astype(F32) for a in _split3(c * LOG2E))
    lane = lax.broadcasted_iota(jnp.int32, (n, LANES), 1)
    packed = jnp.where(lane < H, hi2,
                       jnp.where(lane < 2 * H, pltpu.roll(mid2, H, 1),
                                 jnp.where(lane < 3 * H, pltpu.roll(lo2, 2 * H, 1), 0.0)))
    is_one = (lane >= FOX_ONES_ROW) & (lane < FOX_ONES_ROW + 3)
    kaug_ref[...] = jnp.where(is_one, 1.0, -packed).astype(BF16)
    ct_ref[...] = packed.T


def fox_cumsum(f_logit):
    T = f_logit.shape[0]
    n = FOX_CUM_TILE
    spec = pl.BlockSpec((n, LANES), lambda i: (i, 0))
    return pl.pallas_call(
        _fox_cumsum_kernel,
        grid=(T // n,),
        in_specs=[spec],
        out_specs=[spec, pl.BlockSpec((LANES, n), lambda i: (0, i))],
        out_shape=[jax.ShapeDtypeStruct((T, LANES), BF16), jax.ShapeDtypeStruct((LANES, T), F32)],
        scratch_shapes=[pltpu.VMEM((1, LANES), F32)],
        compiler_params=_cparams(("arbitrary",)),
        name="fox_cumsum",
    )(f_logit)


FOX_TILE = 512
FOX_KEY_TILE = 256


FOX_VROWS = FOX_HEAD_DIM + 16


def _fox_kernel(first_ref, k_ref, kaug_ref, qt_ref, ct_ref, vt_ref, o_ref,
                rhs_ref, acc_ref, m_ref, alpha_ref, st_ref, pt_ref):
    p = pl.program_id(0)
    i = pl.program_id(1)
    tq, tk = FOX_TILE, FOX_KEY_TILE
    dh = FOX_HEAD_DIM
    H = FOX_HEADS

    row = lax.broadcasted_iota(jnp.int32, (LANES, tq), 0)
    qt = qt_ref[...].astype(F32)
    for h in range(2):
        head = 2 * p + h
        q_part = jnp.where((row >= h * dh) & (row < (h + 1) * dh), qt, 0.0)
        gate_rows = (row == head) | (row == H + head) | (row == 2 * H + head)
        c_parts = [ct_ref[pl.ds(part * H + head, 1), :] for part in range(3)]
        aug = jnp.where(gate_rows, 1.0, 0.0)
        for part in range(3):
            aug = jnp.where(row == FOX_ONES_ROW + part, c_parts[part], aug)
        rhs_ref[h] = jnp.concatenate([q_part, aug], axis=0).astype(BF16)

    acc_ref[...] = jnp.zeros_like(acc_ref)
    m_ref[...] = jnp.full_like(m_ref, NEG_BIG)
    pt_ref[1] = jnp.zeros(pt_ref.shape[1:], BF16)
    alpha_ref[1] = jnp.ones(alpha_ref.shape[1:], F32)

    def logits(j, slot):
        start = pl.multiple_of(j * tk, tk)
        lhs = jnp.concatenate([k_ref[pl.ds(start, tk), :], kaug_ref[pl.ds(start, tk), :]], axis=1)
        for h in range(2):
            st_ref[slot, h] = jnp.dot(lhs, rhs_ref[h], preferred_element_type=F32)

    def softmax(slot, mask_offset=None):
        for h in range(2):
            st = st_ref[slot, h]
            if mask_offset is not None:
                causal = (lax.broadcasted_iota(jnp.int32, (tk, tq), 0) + mask_offset
                          <= lax.broadcasted_iota(jnp.int32, (tk, tq), 1))
                st = jnp.where(causal, st, NEG_BIG)
            m_old = m_ref[h]
            m_new = jnp.maximum(m_old, jnp.max(st, axis=0, keepdims=True))
            alpha_ref[slot, h] = jnp.exp2(m_old - m_new)
            pt_ref[slot, h] = jnp.exp2(st - m_new).astype(BF16)
            m_ref[h] = m_new

    ones_rows = jnp.ones((FOX_VROWS - dh, tk), BF16)

    def accum(j, slot):
        start = pl.multiple_of(j * tk, tk)
        for h in range(2):
            vt_aug = jnp.concatenate([vt_ref[h * dh:(h + 1) * dh, pl.ds(start, tk)], ones_rows], axis=0)
            pv = jnp.dot(vt_aug, pt_ref[slot, h], preferred_element_type=F32)
            acc_ref[h] = alpha_ref[slot, h] * acc_ref[h] + pv

    sub = tq // tk

    def body(p, carry):
        accum(jnp.maximum(sub * p - 1, 0), 1)
        softmax(0)
        logits(sub * p + 1, 1)
        accum(sub * p, 0)
        softmax(1)
        logits(sub * p + 2, 0)
        return carry

    first = first_ref[p, i]
    logits(sub * first, 0)
    lax.fori_loop(first, i, body, 0)
    logits(sub * i + 1, 1)
    accum(jnp.maximum(sub * i - 1, 0), 1)
    softmax(0, mask_offset=0)
    accum(sub * i, 0)
    softmax(1, mask_offset=tk)
    accum(sub * i + 1, 1)

    acc_a, acc_b = acc_ref[0], acc_ref[1]
    ot = jnp.concatenate([acc_a[:dh] / acc_a[dh:dh + 1], acc_b[:dh] / acc_b[dh:dh + 1]], axis=0)
    o_ref[...] = ot.T.astype(o_ref.dtype)


def _fox_norms_kernel(k_ref, qt_ref, kn_ref, qn_ref):
    n = FOX_HEADS * FOX_HEAD_DIM
    chan = lax.broadcasted_iota(jnp.int32, (n, LANES), 0) // FOX_HEAD_DIM
    head = lax.broadcasted_iota(jnp.int32, (n, LANES), 1)
    group = (chan == head).astype(BF16)
    k = k_ref[...].astype(F32)
    k2 = jnp.dot((k * k).astype(BF16), group, preferred_element_type=F32)
    q = qt_ref[...].astype(F32)
    q2 = lax.dot_general((q * q).astype(BF16), group, (((0,), (0,)), ((), ())), preferred_element_type=F32)
    kn_ref[0] = jnp.broadcast_to(jnp.max(k2, axis=0, keepdims=True), kn_ref.shape[1:])
    qn_ref[0] = jnp.broadcast_to(jnp.max(q2, axis=0, keepdims=True), qn_ref.shape[1:])


def fox_norms(k, qt):
    T = k.shape[0]
    t = FOX_TILE
    n = FOX_HEADS * FOX_HEAD_DIM
    out = pl.BlockSpec((1, 8, LANES), lambda i: (i, 0, 0))
    return pl.pallas_call(
        _fox_norms_kernel,
        grid=(T // t,),
        in_specs=[pl.BlockSpec((t, n), lambda i: (i, 0)), pl.BlockSpec((n, t), lambda i: (0, i))],
        out_specs=[out, out],
        out_shape=[jax.ShapeDtypeStruct((T // t, 8, LANES), F32)] * 2,
        compiler_params=_cparams(("parallel",)),
        name="fox_norms",
    )(k, qt)


FOX_UNDERFLOW_LOG2 = 160.0


def fox_first_block(kn2, qn2, ct):
    H = FOX_HEADS
    nt = kn2.shape[0]
    bk = jnp.sqrt(jnp.max(kn2[:, 0, :H], axis=0))
    bq = jnp.sqrt(qn2[:, 0, :H])
    need = 2.0 * 1.02 * bq * bk[None, :] + FOX_UNDERFLOW_LOG2 + 2.0
    c_first = ct[:, 0::FOX_TILE]
    c_last = ct[:, FOX_TILE - 1::FOX_TILE]
    c_first = c_first[:H] + c_first[H:2 * H] + c_first[2 * H:3 * H]
    c_last = c_last[:H] + c_last[H:2 * H] + c_last[2 * H:3 * H]
    gap = c_last[:, None, :] - c_first[:, :, None]
    earlier = jnp.arange(nt)[None, :] < jnp.arange(nt)[:, None]
    prunable = (gap > need.T[:, :, None]) & earlier[None]
    n_skip = jnp.min(jnp.where(prunable, nt, jnp.arange(nt)[None, None, :]), axis=2)
    return jnp.min(n_skip.reshape(H // 2, 2, nt), axis=1).astype(jnp.int32)


def fox_attention(first, k, kaug, qt, ct, vt):
    T = k.shape[0]
    t = FOX_TILE
    pairs = FOX_HEADS // 2
    return pl.pallas_call(
        _fox_kernel,
        grid=(pairs, T // t),
        in_specs=[
            pl.BlockSpec(memory_space=pltpu.SMEM),
            pl.BlockSpec((T, LANES), lambda p, i: (0, p)),
            pl.BlockSpec((T, LANES), lambda p, i: (0, 0)),
            pl.BlockSpec((LANES, t), lambda p, i: (p, i)),
            pl.BlockSpec((LANES, t), lambda p, i: (0, i)),
            pl.BlockSpec((LANES, T), lambda p, i: (p, 0)),
        ],
        out_specs=pl.BlockSpec((t, LANES), lambda p, i: (i, p)),
        out_shape=jax.ShapeDtypeStruct((T, FOX_HEADS * FOX_HEAD_DIM), BF16),
        scratch_shapes=[
            pltpu.VMEM((2, 2 * LANES, t), BF16),
            pltpu.VMEM((2, FOX_VROWS, t), F32),
            pltpu.VMEM((2, 1, t), F32),
            pltpu.VMEM((2, 2, 1, t), F32),
            pltpu.VMEM((2, 2, FOX_KEY_TILE, t), F32),
            pltpu.VMEM((2, 2, FOX_KEY_TILE, t), BF16),
        ],
        compiler_params=_cparams(("parallel", "arbitrary")),
        name="fox",
    )(first, k, kaug, qt, ct, vt)


def kernel(x, norm_mix, norm_mlp, norm_final, w_up, w_down, swa_w_qkv, swa_b_qkv, swa_sinks, swa_w_o,
           hgrn_w_in, hgrn_lb_logits, hgrn_g_norm, hgrn_w_o, fox_w_in, fox_b_in, fox_w_o):
    B, T, D = x.shape
    depth = norm_mix.shape[0]
    lb_soft = jax.nn.softmax(hgrn_lb_logits.astype(F32), axis=0)
    lower_bounds = jnp.cumsum(lb_soft, axis=0) - lb_soft[0]

    outs = []
    for b in range(B):
        xb = x[b]
        for i in range(depth):
            m, j = i % N_MIXERS, i // N_MIXERS
            if m == 0:
                nq = SWA_HEADS * SWA_HEAD_DIM
                nkv = SWA_KV_HEADS * SWA_HEAD_DIM
                col_scale = jnp.concatenate([jnp.full((nq,), LOG2E * SWA_HEAD_DIM ** -0.5, F32),
                                             jnp.ones((2 * nkv,), F32)])
                qt, k, vt = norm_proj(xb, norm_mix[i], (swa_w_qkv[j] * col_scale).astype(BF16),
                                      swa_b_qkv[j] * col_scale,
                                      [(0, nq), (nq, nq + nkv), (nq + nkv, nq + 2 * nkv)], [BF16] * 3, "swa_proj",
                                      transposed=(True, False, True))
                a = swa_attention(qt, k, vt, swa_sinks[j])
                w_o = swa_w_o[j]
            elif m == 1:
                hk = HGRN_HEADS * HGRN_EXPAND
                hv = HGRN_HEADS * HGRN_HEAD_DIM
                q, f, v, g = norm_proj(xb, norm_mix[i], hgrn_w_in[j].astype(BF16), jnp.zeros((2 * hk + 2 * hv,), F32),
                                       [(0, hk), (hk, 2 * hk), (2 * hk, 2 * hk + hv), (2 * hk + hv, 2 * hk + 2 * hv)],
                                       [F32, F32, BF16, F32], "hgrn_proj")
                o = hgrn_recurrence(q, f, v, lower_bounds[i])
                a = hgrn_gate(o, g, hgrn_g_norm[j])
                w_o = hgrn_w_o[j]
            else:
                n = FOX_HEADS * FOX_HEAD_DIM
                pad = LANES - FOX_HEADS
                col_scale = jnp.concatenate([jnp.full((n,), LOG2E * FOX_HEAD_DIM ** -0.5, F32),
                                             jnp.ones((2 * n + FOX_HEADS,), F32)])
                w = jnp.pad(fox_w_in[j] * col_scale, ((0, 0), (0, pad))).astype(BF16)
                bias = jnp.pad(fox_b_in[j] * col_scale, ((0, pad),))
                qt, k, vt, f = norm_proj(xb, norm_mix[i], w, bias,
                                         [(0, n), (n, 2 * n), (2 * n, 3 * n), (3 * n, 3 * n + LANES)],
                                         [BF16, BF16, BF16, F32], "fox_proj",
                                         transposed=(True, False, True, False))
                kaug, ct = fox_cumsum(f)
                first = fox_first_block(*fox_norms(k, qt), ct)
                a = fox_attention(first, k, kaug, qt, ct, vt)
                w_o = fox_w_o[j]
            last = i == depth - 1
            xb = out_mlp(a, w_o.astype(BF16), xb, norm_mlp[i], w_up[i].astype(BF16), w_down[i].astype(BF16),
                         norm_final, last)
        outs.append(xb)
    return jnp.stack(outs, axis=0)
```

```python
import functools

import jax
import jax.numpy as jnp
from jax import lax
from jax.experimental import pallas as pl
from jax.experimental.pallas import tpu as pltpu

F32 = jnp.float32
BF16 = jnp.bfloat16

D_MODEL = 1024
D_FF = 4 * D_MODEL
EPS = 1e-6
N_MIXERS = 3

SWA_HEADS = 16
SWA_KV_HEADS = 4
SWA_HEAD_DIM = 64
SWA_WINDOW = 128

HGRN_HEADS = 8
HGRN_EXPAND = 128
HGRN_HEAD_DIM = 128
HGRN_CHUNK = 64
HGRN_SUB = 8
HGRN_STEP_CHUNKS = 2

FOX_HEADS = 16
FOX_HEAD_DIM = 64

LANES = 128
VMEM_LIMIT = 56 * 1024 * 1024
NEG_BIG = -1e30
LOG2E = 1.4426950408889634

ROW_TILE = 1024


def _cparams(sem):
    return pltpu.CompilerParams(dimension_semantics=sem, vmem_limit_bytes=VMEM_LIMIT)


def _const_spec(shape):
    nd = len(shape)
    return pl.BlockSpec(shape, lambda *_: (0,) * nd, pipeline_mode=pl.Buffered(1))


def _rms(x, g):
    ms = jnp.mean(x * x, axis=-1, keepdims=True)
    return x * lax.rsqrt(ms + EPS) * g


def _norm_proj_kernel(x_ref, g_ref, w_ref, b_ref, *o_refs, segments, transposed, n_chunk):
    h = _rms(x_ref[...], g_ref[...]).astype(BF16)
    for o_ref, (lo, hi), tr in zip(o_refs, segments, transposed):
        for j in range(lo, hi, n_chunk):
            je = min(j + n_chunk, hi)
            y = jnp.dot(h, w_ref[:, j:je], preferred_element_type=F32) + b_ref[:, j:je]
            if tr:
                o_ref[j - lo:je - lo, :] = y.T.astype(o_ref.dtype)
            else:
                o_ref[:, j - lo:je - lo] = y.astype(o_ref.dtype)


def norm_proj(x, g, w, b, segments, dtypes, name, transposed=None):
    T, D = x.shape
    N = w.shape[1]
    tm = ROW_TILE
    transposed = tuple(transposed) if transposed is not None else (False,) * len(segments)
    out_shape, out_specs = [], []
    for (lo, hi), dt, tr in zip(segments, dtypes, transposed):
        if tr:
            out_shape.append(jax.ShapeDtypeStruct((hi - lo, T), dt))
            out_specs.append(pl.BlockSpec((hi - lo, tm), lambda i: (0, i)))
        else:
            out_shape.append(jax.ShapeDtypeStruct((T, hi - lo), dt))
            out_specs.append(pl.BlockSpec((tm, hi - lo), lambda i: (i, 0)))
    return pl.pallas_call(
        functools.partial(_norm_proj_kernel, segments=tuple(segments), transposed=transposed, n_chunk=512),
        grid=(T // tm,),
        in_specs=[
            pl.BlockSpec((tm, D), lambda i: (i, 0)),
            _const_spec((1, D)),
            _const_spec((D, N)),
            _const_spec((1, N)),
        ],
        out_specs=out_specs,
        out_shape=out_shape,
        compiler_params=_cparams(("parallel",)),
        name=name,
    )(x, g.reshape(1, D), w, b.reshape(1, N))


def _hgrn_gate_kernel(o_ref, g_ref, gn_ref, a_ref):
    for h in range(HGRN_HEADS):
        sl = slice(h * HGRN_HEAD_DIM, (h + 1) * HGRN_HEAD_DIM)
        o = o_ref[:, sl]
        o = o * lax.rsqrt(jnp.mean(o * o, axis=-1, keepdims=True) + EPS)
        g = g_ref[:, sl]
        a_ref[:, sl] = (o * gn_ref[:, sl] * (g * jax.nn.sigmoid(g))).astype(a_ref.dtype)


def hgrn_gate(o, g, g_norm):
    T, D = o.shape
    tm = ROW_TILE
    row = pl.BlockSpec((tm, D), lambda i: (i, 0))
    return pl.pallas_call(
        _hgrn_gate_kernel,
        grid=(T // tm,),
        in_specs=[row, row, _const_spec((1, D))],
        out_specs=row,
        out_shape=jax.ShapeDtypeStruct((T, D), BF16),
        compiler_params=_cparams(("parallel",)),
        name="hgrn_gate",
    )(o, g, g_norm.reshape(1, D))


def _out_mlp_kernel(a_ref, wo_ref, x_ref, g_ref, wu_ref, wd_ref, gf_ref, o_ref, *, ff_chunk, final_norm):
    x = x_ref[...] + jnp.dot(a_ref[...], wo_ref[...], preferred_element_type=F32)
    h = _rms(x, g_ref[...]).astype(BF16)
    acc = x
    for j in range(0, D_FF, ff_chunk):
        u = jnp.dot(h, wu_ref[:, j:j + ff_chunk], preferred_element_type=F32)
        u = jnp.maximum(u, 0.0)
        u = (u * u).astype(BF16)
        acc = acc + jnp.dot(u, wd_ref[j:j + ff_chunk, :], preferred_element_type=F32)
    if final_norm:
        acc = _rms(acc, gf_ref[...])
    o_ref[...] = acc


def out_mlp(a, w_o, x, g, w_up, w_down, g_final, final_norm):
    T, D = x.shape
    tm = ROW_TILE
    row = pl.BlockSpec((tm, D), lambda i: (i, 0))
    return pl.pallas_call(
        functools.partial(_out_mlp_kernel, ff_chunk=512, final_norm=final_norm),
        grid=(T // tm,),
        in_specs=[
            row,
            _const_spec((D, D)),
            row,
            _const_spec((1, D)),
            _const_spec((D, D_FF)),
            _const_spec((D_FF, D)),
            _const_spec((1, D)),
        ],
        out_specs=row,
        out_shape=jax.ShapeDtypeStruct((T, D), F32),
        compiler_params=_cparams(("parallel",)),
        name="out_mlp",
    )(a, w_o, x, g.reshape(1, D), w_up, w_down, g_final.reshape(1, D))


SWA_VROWS = SWA_HEAD_DIM + 16
SWA_STEP_BLOCKS = 4


def _swa_kernel(sink_ref, qt_ref, kc_ref, kp_ref, vtc_ref, vtp_ref, o_ref):
    n = pl.program_id(0)
    W, dh, NB = SWA_WINDOW, SWA_HEAD_DIM, SWA_STEP_BLOCKS
    G = SWA_HEADS // SWA_KV_HEADS
    key = lax.broadcasted_iota(jnp.int32, (2 * W, G * W), 0)
    lane = lax.broadcasted_iota(jnp.int32, (2 * W, G * W), 1)
    rel = lane % W + W - key
    in_window = (rel >= 0) & (rel < W)
    group_lane = lax.broadcasted_iota(jnp.int32, (1, G * W), 1) // W
    k_all = jnp.concatenate([kp_ref[...], kc_ref[...]], axis=0)
    vt_all = jnp.concatenate([vtp_ref[...], vtc_ref[...]], axis=1)
    ones_rows = jnp.ones((SWA_VROWS - dh, 2 * W), BF16)
    units = [(b, hk) for b in range(NB) for hk in range(SWA_KV_HEADS)]

    sts = []
    for b, hk in units:
        pair, half = divmod(hk, 2)
        lhs = k_all[b * W:(b + 2) * W, pair * LANES:(pair + 1) * LANES]
        q_cat = jnp.concatenate([qt_ref[(hk * G + g) * dh:(hk * G + g + 1) * dh, b * W:(b + 1) * W]
                                 for g in range(G)], axis=1)
        zeros = jnp.zeros_like(q_cat)
        rhs = jnp.concatenate([q_cat, zeros] if half == 0 else [zeros, q_cat], axis=0)
        sts.append(jnp.dot(lhs, rhs, preferred_element_type=F32))

    pts, sink_terms = [], []
    for (b, hk), st in zip(units, sts):
        visible = in_window if b > 0 else in_window & ((n > 0) | (key >= W))
        st = jnp.where(visible, st, NEG_BIG)
        sink = jnp.zeros((1, G * W), F32)
        for g in range(G):
            sink = jnp.where(group_lane == g, sink_ref[hk * G + g] * LOG2E, sink)
        m = jnp.maximum(jnp.max(st, axis=0, keepdims=True), sink)
        pts.append(jnp.exp2(st - m).astype(BF16))
        sink_terms.append(jnp.exp2(sink - m))

    accs = []
    for (b, hk), pt in zip(units, pts):
        vt_aug = jnp.concatenate([vt_all[hk * dh:(hk + 1) * dh, b * W:(b + 2) * W], ones_rows], axis=0)
        accs.append(jnp.dot(vt_aug, pt, preferred_element_type=F32))

    outs = [[] for _ in range(NB)]
    for (b, hk), acc, sink_term in zip(units, accs, sink_terms):
        ot = acc[:dh] / (acc[dh:dh + 1] + sink_term)
        for g in range(0, G, 2):
            two = jnp.concatenate([ot[:, g * W:(g + 1) * W], ot[:, (g + 1) * W:(g + 2) * W]], axis=0)
            outs[b].append(two.T)
    for b in range(NB):
        o_ref[b * W:(b + 1) * W, :] = jnp.concatenate(outs[b], axis=1).astype(o_ref.dtype)


def swa_attention(qt, k, vt, sinks):
    T = k.shape[0]
    W, NB = SWA_WINDOW, SWA_STEP_BLOCKS
    nq = SWA_HEADS * SWA_HEAD_DIM
    kvw = SWA_KV_HEADS * SWA_HEAD_DIM
    prev = lambda n: jnp.maximum(NB * n - 1, 0)
    return pl.pallas_call(
        _swa_kernel,
        grid=(T // (NB * W),),
        in_specs=[
            pl.BlockSpec(memory_space=pltpu.SMEM),
            pl.BlockSpec((nq, NB * W), lambda n: (0, n)),
            pl.BlockSpec((NB * W, kvw), lambda n: (n, 0)),
            pl.BlockSpec((W, kvw), lambda n: (prev(n), 0)),
            pl.BlockSpec((kvw, NB * W), lambda n: (0, n)),
            pl.BlockSpec((kvw, W), lambda n: (0, prev(n))),
        ],
        out_specs=pl.BlockSpec((NB * W, nq), lambda n: (n, 0)),
        out_shape=jax.ShapeDtypeStruct((T, nq), BF16),
        compiler_params=_cparams(("parallel",)),
        name="swa",
    )(sinks, qt, k, k, vt, vt)


def _split2(x):
    hi = x.astype(BF16)
    lo = (x - hi.astype(F32)).astype(BF16)
    return hi, lo


def _split_dot(m_bf16, parts):
    hi, lo = parts
    return (jnp.dot(m_bf16, hi, preferred_element_type=F32)
            + jnp.dot(m_bf16, lo, preferred_element_type=F32))


def _hgrn_kernel(q_ref, f_ref, v_ref, lb_ref, o_ref, st_ref):
    C, SUB, K, H = HGRN_CHUNK, HGRN_SUB, HGRN_EXPAND, HGRN_HEADS
    R = C * HGRN_STEP_CHUNKS
    nt = (((1,), (1,)), ((), ()))
    tn = (((0,), (0,)), ((), ()))

    @pl.when(pl.program_id(0) == 0)
    def _():
        st_ref[...] = jnp.zeros_like(st_ref)

    row = lax.broadcasted_iota(jnp.int32, (R, R), 0)
    col = lax.broadcasted_iota(jnp.int32, (R, R), 1)
    same_chunk = row // C == col // C
    heads = [slice(h * K, (h + 1) * K) for h in range(H)]
    chunks = [slice(c * C, (c + 1) * C) for c in range(HGRN_STEP_CHUNKS)]

    lb = lb_ref[...]
    f = lb + (1.0 - lb) * jax.nn.sigmoid(f_ref[...])
    logf = _split2(jnp.log2(f))
    kk = 1.0 - f
    q = q_ref[...]
    qs = q * jax.nn.sigmoid(q)
    v = v_ref[...]
    bcum = _split_dot(((row >= col) & same_chunk).astype(BF16), logf)
    to_end = _split_dot(((row < col) & same_chunk).astype(BF16), logf)

    q_in = (qs * jnp.exp2(bcum)).astype(BF16)
    k_out = (kk * jnp.exp2(to_end)).astype(BF16)
    o_in = [[] for _ in heads]
    for cs in chunks:
        decay = jnp.exp2(bcum[cs.stop - 1:cs.stop, :])
        for h, hs in enumerate(heads):
            st = st_ref[h]
            o_in[h].append(lax.dot_general(q_in[cs, hs], st.astype(BF16), nt, preferred_element_type=F32))
            upd = lax.dot_general(v[cs, hs], k_out[cs, hs], tn, preferred_element_type=F32)
            st_ref[h] = st * decay[:, hs] + upd
    o = [jnp.concatenate(parts, axis=0) for parts in o_in]


    a = [jnp.zeros((R, R), F32) for _ in heads]
    half = SUB
    while half < C:
        pos = row % (2 * half)
        bnd = row - pos + half - 1
        to_bnd = (((pos >= half) & (col > bnd) & (col <= row))
                  | ((pos < half) & (col > row) & (col <= bnd)))
        e = jnp.exp2(_split_dot(to_bnd.astype(BF16), logf))
        q_e = (qs * e).astype(BF16)
        k_e = (kk * e).astype(BF16)
        right_left = (pos >= half) & (col <= bnd) & (col > bnd - half)
        for h, hs in enumerate(heads):
            a_l = lax.dot_general(q_e[:, hs], k_e[:, hs], nt, preferred_element_type=F32)
            a[h] = jnp.where(right_left, a_l, a[h])
        half *= 2

    sub_pos = lax.broadcasted_iota(jnp.int32, (R, H * K), 0) % SUB
    for d in range(SUB):
        if d == 0:
            e = qs * kk
        else:
            e = qs * pltpu.roll(kk, d, 0) * jnp.exp2(bcum - pltpu.roll(bcum, d, 0))
            e = jnp.where(sub_pos >= d, e, 0.0)
        band = (row - col == d) & (row % SUB >= d)
        for h, hs in enumerate(heads):
            a[h] = jnp.where(band, jnp.sum(e[:, hs], axis=1, keepdims=True), a[h])

    for h, hs in enumerate(heads):
        o[h] = o[h] + jnp.dot(a[h].astype(BF16), v[:, hs], preferred_element_type=F32)
    o_ref[...] = jnp.concatenate(o, axis=1)


def hgrn_recurrence(q, f_logit, v, lb):
    T = q.shape[0]
    rows = HGRN_CHUNK * HGRN_STEP_CHUNKS
    width = HGRN_HEADS * HGRN_EXPAND
    blk = pl.BlockSpec((rows, width), lambda c: (c, 0))
    return pl.pallas_call(
        _hgrn_kernel,
        grid=(T // rows,),
        in_specs=[blk, blk, blk, _const_spec((1, width))],
        out_specs=blk,
        out_shape=jax.ShapeDtypeStruct((T, width), F32),
        scratch_shapes=[pltpu.VMEM((HGRN_HEADS, HGRN_HEAD_DIM, HGRN_EXPAND), F32)],
        compiler_params=_cparams(("arbitrary",)),
        name="hgrn",
    )(q, f_logit, v, lb.reshape(1, -1))


FOX_CUM_TILE = 256


def _split3(x):
    hi = x.astype(BF16)
    r1 = x - hi.astype(F32)
    mid = r1.astype(BF16)
    lo = (r1 - mid.astype(F32)).astype(BF16)
    return hi, mid, lo


FOX_ONES_ROW = 3 * FOX_HEADS


def _fox_cumsum_kernel(f_ref, kaug_ref, ct_ref, carry_ref):
    @pl.when(pl.program_id(0) == 0)
    def _():
        carry_ref[...] = jnp.zeros_like(carry_ref)

    x = f_ref[...]
    ls = jnp.minimum(x, 0.0) - jnp.log(1.0 + jnp.exp(-jnp.abs(x)))
    n = FOX_CUM_TILE
    tril = (lax.broadcasted_iota(jnp.int32, (n, n), 0) >= lax.broadcasted_iota(jnp.int32, (n, n), 1)).astype(BF16)
    hi, mid, lo = _split3(ls)
    cum = (jnp.dot(tril, hi, preferred_element_type=F32)
           + jnp.dot(tril, mid, preferred_element_type=F32)
           + jnp.dot(tril, lo, preferred_element_type=F32))
    c = cum + carry_ref[...]
    carry_ref[...] = c[n - 1:n, :]
    H = FOX_HEADS
    hi2, mid2, lo2 = (a.astype(F32) for a in _split3(c * LOG2E))
    lane = lax.broadcasted_iota(jnp.int32, (n, LANES), 1)
    packed = jnp.where(lane < H, hi2,
                       jnp.where(lane < 2 * H, pltpu.roll(mid2, H, 1),
                                 jnp.where(lane < 3 * H, pltpu.roll(lo2, 2 * H, 1), 0.0)))
    is_one = (lane >= FOX_ONES_ROW) & (lane < FOX_ONES_ROW + 3)
    kaug_ref[...] = jnp.where(is_one, 1.0, -packed).astype(BF16)
    ct_ref[...] = packed.T


def fox_cumsum(f_logit):
    T = f_logit.shape[0]
    n = FOX_CUM_TILE
    spec = pl.BlockSpec((n, LANES), lambda i: (i, 0))
    return pl.pallas_call(
        _fox_cumsum_kernel,
        grid=(T // n,),
        in_specs=[spec],
        out_specs=[spec, pl.BlockSpec((LANES, n), lambda i: (0, i))],
        out_shape=[jax.ShapeDtypeStruct((T, LANES), BF16), jax.ShapeDtypeStruct((LANES, T), F32)],
        scratch_shapes=[pltpu.VMEM((1, LANES), F32)],
        compiler_params=_cparams(("arbitrary",)),
        name="fox_cumsum",
    )(f_logit)


FOX_TILE = 512
FOX_KEY_TILE = 256


FOX_VROWS = FOX_HEAD_DIM + 16


def _fox_kernel(first_ref, k_ref, kaug_ref, qt_ref, ct_ref, vt_ref, o_ref,
                rhs_ref, acc_ref, m_ref, alpha_ref, st_ref, pt_ref):
    p = pl.program_id(0)
    i = pl.program_id(1)
    tq, tk = FOX_TILE, FOX_KEY_TILE
    dh = FOX_HEAD_DIM
    H = FOX_HEADS

    row = lax.broadcasted_iota(jnp.int32, (LANES, tq), 0)
    qt = qt_ref[...].astype(F32)
    for h in range(2):
        head = 2 * p + h
        q_part = jnp.where((row >= h * dh) & (row < (h + 1) * dh), qt, 0.0)
        gate_rows = (row == head) | (row == H + head) | (row == 2 * H + head)
        c_parts = [ct_ref[pl.ds(part * H + head, 1), :] for part in range(3)]
        aug = jnp.where(gate_rows, 1.0, 0.0)
        for part in range(3):
            aug = jnp.where(row == FOX_ONES_ROW + part, c_parts[part], aug)
        rhs_ref[h] = jnp.concatenate([q_part, aug], axis=0).astype(BF16)

    acc_ref[...] = jnp.zeros_like(acc_ref)
    m_ref[...] = jnp.full_like(m_ref, NEG_BIG)
    pt_ref[1] = jnp.zeros(pt_ref.shape[1:], BF16)
    alpha_ref[1] = jnp.ones(alpha_ref.shape[1:], F32)

    def logits(j, slot):
        start = pl.multiple_of(j * tk, tk)
        lhs = jnp.concatenate([k_ref[pl.ds(start, tk), :], kaug_ref[pl.ds(start, tk), :]], axis=1)
        for h in range(2):
            st_ref[slot, h] = jnp.dot(lhs, rhs_ref[h], preferred_element_type=F32)

    def softmax(slot, mask_offset=None):
        for h in range(2):
            st = st_ref[slot, h]
            if mask_offset is not None:
                causal = (lax.broadcasted_iota(jnp.int32, (tk, tq), 0) + mask_offset
                          <= lax.broadcasted_iota(jnp.int32, (tk, tq), 1))
                st = jnp.where(causal, st, NEG_BIG)
            m_old = m_ref[h]
            m_new = jnp.maximum(m_old, jnp.max(st, axis=0, keepdims=True))
            alpha_ref[slot, h] = jnp.exp2(m_old - m_new)
            pt_ref[slot, h] = jnp.exp2(st - m_new).astype(BF16)
            m_ref[h] = m_new

    ones_rows = jnp.ones((FOX_VROWS - dh, tk), BF16)

    def accum(j, slot):
        start = pl.multiple_of(j * tk, tk)
        for h in range(2):
            vt_aug = jnp.concatenate([vt_ref[h * dh:(h + 1) * dh, pl.ds(start, tk)], ones_rows], axis=0)
            pv = jnp.dot(vt_aug, pt_ref[slot, h], preferred_element_type=F32)
            acc_ref[h] = alpha_ref[slot, h] * acc_ref[h] + pv

    sub = tq // tk

    def body(p, carry):
        accum(jnp.maximum(sub * p - 1, 0), 1)
        softmax(0)
        logits(sub * p + 1, 1)
        accum(sub * p, 0)
        softmax(1)
        logits(sub * p + 2, 0)
        return carry

    first = first_ref[p, i]
    logits(sub * first, 0)
    lax.fori_loop(first, i, body, 0)
    logits(sub * i + 1, 1)
    accum(jnp.maximum(sub * i - 1, 0), 1)
    softmax(0, mask_offset=0)
    accum(sub * i, 0)
    softmax(1, mask_offset=tk)
    accum(sub * i + 1, 1)

    acc_a, acc_b = acc_ref[0], acc_ref[1]
    ot = jnp.concatenate([acc_a[:dh] / acc_a[dh:dh + 1], acc_b[:dh] / acc_b[dh:dh + 1]], axis=0)
    o_ref[...] = ot.T.astype(o_ref.dtype)


def _fox_norms_kernel(k_ref, qt_ref, kn_ref, qn_ref):
    n = FOX_HEADS * FOX_HEAD_DIM
    chan = lax.broadcasted_iota(jnp.int32, (n, LANES), 0) // FOX_HEAD_DIM
    head = lax.broadcasted_iota(jnp.int32, (n, LANES), 1)
    group = (chan == head).astype(BF16)
    k = k_ref[...].astype(F32)
    k2 = jnp.dot((k * k).astype(BF16), group, preferred_element_type=F32)
    q = qt_ref[...].astype(F32)
    q2 = lax.dot_general((q * q).astype(BF16), group, (((0,), (0,)), ((), ())), preferred_element_type=F32)
    kn_ref[0] = jnp.broadcast_to(jnp.max(k2, axis=0, keepdims=True), kn_ref.shape[1:])
    qn_ref[0] = jnp.broadcast_to(jnp.max(q2, axis=0, keepdims=True), qn_ref.shape[1:])


def fox_norms(k, qt):
    T = k.shape[0]
    t = FOX_TILE
    n = FOX_HEADS * FOX_HEAD_DIM
    out = pl.BlockSpec((1, 8, LANES), lambda i: (i, 0, 0))
    return pl.pallas_call(
        _fox_norms_kernel,
        grid=(T // t,),
        in_specs=[pl.BlockSpec((t, n), lambda i: (i, 0)), pl.BlockSpec((n, t), lambda i: (0, i))],
        out_specs=[out, out],
        out_shape=[jax.ShapeDtypeStruct((T // t, 8, LANES), F32)] * 2,
        compiler_params=_cparams(("parallel",)),
        name="fox_norms",
    )(k, qt)


FOX_UNDERFLOW_LOG2 = 160.0


def fox_first_block(kn2, qn2, ct):
    H = FOX_HEADS
    nt = kn2.shape[0]
    bk = jnp.sqrt(jnp.max(kn2[:, 0, :H], axis=0))
    bq = jnp.sqrt(qn2[:, 0, :H])
    need = 2.0 * 1.02 * bq * bk[None, :] + FOX_UNDERFLOW_LOG2 + 2.0
    c_first = ct[:, 0::FOX_TILE]
    c_last = ct[:, FOX_TILE - 1::FOX_TILE]
    c_first = c_first[:H] + c_first[H:2 * H] + c_first[2 * H:3 * H]
    c_last = c_last[:H] + c_last[H:2 * H] + c_last[2 * H:3 * H]
    gap = c_last[:, None, :] - c_first[:, :, None]
    earlier = jnp.arange(nt)[None, :] < jnp.arange(nt)[:, None]
    prunable = (gap > need.T[:, :, None]) & earlier[None]
    n_skip = jnp.min(jnp.where(prunable, nt, jnp.arange(nt)[None, None, :]), axis=2)
    return jnp.min(n_skip.reshape(H // 2, 2, nt), axis=1).astype(jnp.int32)


def fox_attention(first, k, kaug, qt, ct, vt):
    T = k.shape[0]
    t = FOX_TILE
    pairs = FOX_HEADS // 2
    return pl.pallas_call(
        _fox_kernel,
        grid=(pairs, T // t),
        in_specs=[
            pl.BlockSpec(memory_space=pltpu.SMEM),
            pl.BlockSpec((T, LANES), lambda p, i: (0, p)),
            pl.BlockSpec((T, LANES), lambda p, i: (0, 0)),
            pl.BlockSpec((LANES, t), lambda p, i: (p, i)),
            pl.BlockSpec((LANES, t), lambda p, i: (0, i)),
            pl.BlockSpec((LANES, T), lambda p, i: (p, 0)),
        ],
        out_specs=pl.BlockSpec((t, LANES), lambda p, i: (i, p)),
        out_shape=jax.ShapeDtypeStruct((T, FOX_HEADS * FOX_HEAD_DIM), BF16),
        scratch_shapes=[
            pltpu.VMEM((2, 2 * LANES, t), BF16),
            pltpu.VMEM((2, FOX_VROWS, t), F32),
            pltpu.VMEM((2, 1, t), F32),
            pltpu.VMEM((2, 2, 1, t), F32),
            pltpu.VMEM((2, 2, FOX_KEY_TILE, t), F32),
            pltpu.VMEM((2, 2, FOX_KEY_TILE, t), BF16),
        ],
        compiler_params=_cparams(("parallel", "arbitrary")),
        name="fox",
    )(first, k, kaug, qt, ct, vt)


def kernel(x, norm_mix, norm_mlp, norm_final, w_up, w_down, swa_w_qkv, swa_b_qkv, swa_sinks, swa_w_o,
           hgrn_w_in, hgrn_lb_logits, hgrn_g_norm, hgrn_w_o, fox_w_in, fox_b_in, fox_w_o):
    B, T, D = x.shape
    depth = norm_mix.shape[0]
    lb_soft = jax.nn.softmax(hgrn_lb_logits.astype(F32), axis=0)
    lower_bounds = jnp.cumsum(lb_soft, axis=0) - lb_soft[0]

    outs = []
    for b in range(B):
        xb = x[b]
        for i in range(depth):
            m, j = i % N_MIXERS, i // N_MIXERS
            if m == 0:
                nq = SWA_HEADS * SWA_HEAD_DIM
                nkv = SWA_KV_HEADS * SWA_HEAD_DIM
                col_scale = jnp.concatenate([jnp.full((nq,), LOG2E * SWA_HEAD_DIM ** -0.5, F32),
                                             jnp.ones((2 * nkv,), F32)])
                qt, k, vt = norm_proj(xb, norm_mix[i], (swa_w_qkv[j] * col_scale).astype(BF16),
                                      swa_b_qkv[j] * col_scale,
                                      [(0, nq), (nq, nq + nkv), (nq + nkv, nq + 2 * nkv)], [BF16] * 3, "swa_proj",
                                      transposed=(True, False, True))
                a = swa_attention(qt, k, vt, swa_sinks[j])
                w_o = swa_w_o[j]
            elif m == 1:
                hk = HGRN_HEADS * HGRN_EXPAND
                hv = HGRN_HEADS * HGRN_HEAD_DIM
                q, f, v, g = norm_proj(xb, norm_mix[i], hgrn_w_in[j].astype(BF16), jnp.zeros((2 * hk + 2 * hv,), F32),
                                       [(0, hk), (hk, 2 * hk), (2 * hk, 2 * hk + hv), (2 * hk + hv, 2 * hk + 2 * hv)],
                                       [F32, F32, BF16, F32], "hgrn_proj")
                o = hgrn_recurrence(q, f, v, lower_bounds[i])
                a = hgrn_gate(o, g, hgrn_g_norm[j])
                w_o = hgrn_w_o[j]
            else:
                n = FOX_HEADS * FOX_HEAD_DIM
                pad = LANES - FOX_HEADS
                col_scale = jnp.concatenate([jnp.full((n,), LOG2E * FOX_HEAD_DIM ** -0.5, F32),
                                             jnp.ones((2 * n + FOX_HEADS,), F32)])
                w = jnp.pad(fox_w_in[j] * col_scale, ((0, 0), (0, pad))).astype(BF16)
                bias = jnp.pad(fox_b_in[j] * col_scale, ((0, pad),))
                qt, k, vt, f = norm_proj(xb, norm_mix[i], w, bias,
                                         [(0, n), (n, 2 * n), (2 * n, 3 * n), (3 * n, 3 * n + LANES)],
                                         [BF16, BF16, BF16, F32], "fox_proj",
                                         transposed=(True, False, True, False))
                kaug, ct = fox_cumsum(f)
                first = fox_first_block(*fox_norms(k, qt), ct)
                a = fox_attention(first, k, kaug, qt, ct, vt)
                w_o = fox_w_o[j]
            last = i == depth - 1
            xb = out_mlp(a, w_o.astype(BF16), xb, norm_mlp[i], w_up[i].astype(BF16), w_down[i].astype(BF16),
                         norm_final, last)
        outs.append(xb)
    return jnp.stack(outs, axis=0)
```

```python
import functools

import jax
import jax.numpy as jnp
from jax import lax
from jax.experimental import pallas as pl
from jax.experimental.pallas import tpu as pltpu

F32 = jnp.float32
BF16 = jnp.bfloat16

D_MODEL = 1024
D_FF = 4 * D_MODEL
EPS = 1e-6
N_MIXERS = 3

SWA_HEADS = 16
SWA_KV_HEADS = 4
SWA_HEAD_DIM = 64
SWA_WINDOW = 128

HGRN_HEADS = 8
HGRN_EXPAND = 128
HGRN_HEAD_DIM = 128
HGRN_CHUNK = 64
HGRN_SUB = 8
HGRN_STEP_CHUNKS = 2

FOX_HEADS = 16
FOX_HEAD_DIM = 64

LANES = 128
VMEM_LIMIT = 56 * 1024 * 1024
NEG_BIG = -1e30
LOG2E = 1.4426950408889634

ROW_TILE = 1024


def _cparams(sem):
    return pltpu.CompilerParams(dimension_semantics=sem, vmem_limit_bytes=VMEM_LIMIT)


def _const_spec(shape):
    nd = len(shape)
    return pl.BlockSpec(shape, lambda *_: (0,) * nd, pipeline_mode=pl.Buffered(1))


def _rms(x, g):
    ms = jnp.mean(x * x, axis=-1, keepdims=True)
    return x * lax.rsqrt(ms + EPS) * g


def _norm_proj_kernel(x_ref, g_ref, w_ref, b_ref, *o_refs, segments, transposed, n_chunk):
    h = _rms(x_ref[...], g_ref[...]).astype(BF16)
    for o_ref, (lo, hi), tr in zip(o_refs, segments, transposed):
        for j in range(lo, hi, n_chunk):
            je = min(j + n_chunk, hi)
            y = jnp.dot(h, w_ref[:, j:je], preferred_element_type=F32) + b_ref[:, j:je]
            if tr:
                o_ref[j - lo:je - lo, :] = y.T.astype(o_ref.dtype)
            else:
                o_ref[:, j - lo:je - lo] = y.astype(o_ref.dtype)


def norm_proj(x, g, w, b, segments, dtypes, name, transposed=None):
    T, D = x.shape
    N = w.shape[1]
    tm = ROW_TILE
    transposed = tuple(transposed) if transposed is not None else (False,) * len(segments)
    out_shape, out_specs = [], []
    for (lo, hi), dt, tr in zip(segments, dtypes, transposed):
        if tr:
            out_shape.append(jax.ShapeDtypeStruct((hi - lo, T), dt))
            out_specs.append(pl.BlockSpec((hi - lo, tm), lambda i: (0, i)))
        else:
            out_shape.append(jax.ShapeDtypeStruct((T, hi - lo), dt))
            out_specs.append(pl.BlockSpec((tm, hi - lo), lambda i: (i, 0)))
    return pl.pallas_call(
        functools.partial(_norm_proj_kernel, segments=tuple(segments), transposed=transposed, n_chunk=512),
        grid=(T // tm,),
        in_specs=[
            pl.BlockSpec((tm, D), lambda i: (i, 0)),
            _const_spec((1, D)),
            _const_spec((D, N)),
            _const_spec((1, N)),
        ],
        out_specs=out_specs,
        out_shape=out_shape,
        compiler_params=_cparams(("parallel",)),
        name=name,
    )(x, g.reshape(1, D), w, b.reshape(1, N))


def _hgrn_gate(r_ref, gate_ref, gain_ref):
    parts = []
    for h in range(HGRN_HEADS):
        sl = slice(h * HGRN_HEAD_DIM, (h + 1) * HGRN_HEAD_DIM)
        r = r_ref[:, sl]
        r = r * lax.rsqrt(jnp.mean(r * r, axis=-1, keepdims=True) + EPS)
        g = gate_ref[:, sl]
        parts.append((r * gain_ref[:, sl] * (g * jax.nn.sigmoid(g))).astype(BF16))
    return jnp.concatenate(parts, axis=1)


def _mlp_tail(x, g_ref, wu_ref, wd_ref, gf_ref, o_ref, ff_chunk, final_norm):
    h = _rms(x, g_ref[...]).astype(BF16)
    acc = x
    for j in range(0, D_FF, ff_chunk):
        u = jnp.dot(h, wu_ref[:, j:j + ff_chunk], preferred_element_type=F32)
        u = jnp.maximum(u, 0.0)
        u = (u * u).astype(BF16)
        acc = acc + jnp.dot(u, wd_ref[j:j + ff_chunk, :], preferred_element_type=F32)
    if final_norm:
        acc = _rms(acc, gf_ref[...])
    o_ref[...] = acc


def _out_mlp_kernel(a_ref, wo_ref, x_ref, g_ref, wu_ref, wd_ref, gf_ref, o_ref, *, ff_chunk, final_norm):
    x = x_ref[...] + jnp.dot(a_ref[...], wo_ref[...], preferred_element_type=F32)
    _mlp_tail(x, g_ref, wu_ref, wd_ref, gf_ref, o_ref, ff_chunk, final_norm)


def _gated_out_mlp_kernel(r_ref, gate_ref, gain_ref, wo_ref, x_ref, g_ref, wu_ref, wd_ref, gf_ref, o_ref, *,
                          ff_chunk, final_norm):
    a = _hgrn_gate(r_ref, gate_ref, gain_ref)
    x = x_ref[...] + jnp.dot(a, wo_ref[...], preferred_element_type=F32)
    _mlp_tail(x, g_ref, wu_ref, wd_ref, gf_ref, o_ref, ff_chunk, final_norm)


def out_mlp(mixer_out, w_o, x, g, w_up, w_down, g_final, final_norm):
    T, D = x.shape
    gated = isinstance(mixer_out, tuple)
    tm = ROW_TILE // 2 if gated else ROW_TILE
    row = pl.BlockSpec((tm, D), lambda i: (i, 0))
    if gated:
        r, gate, gain = mixer_out
        body, mix_args, mix_specs = _gated_out_mlp_kernel, (r, gate, gain.reshape(1, D)), [row, row, _const_spec((1, D))]
    else:
        body, mix_args, mix_specs = _out_mlp_kernel, (mixer_out,), [row]
    return pl.pallas_call(
        functools.partial(body, ff_chunk=512, final_norm=final_norm),
        grid=(T // tm,),
        in_specs=mix_specs + [
            _const_spec((D, D)),
            row,
            _const_spec((1, D)),
            _const_spec((D, D_FF)),
            _const_spec((D_FF, D)),
            _const_spec((1, D)),
        ],
        out_specs=row,
        out_shape=jax.ShapeDtypeStruct((T, D), F32),
        compiler_params=_cparams(("parallel",)),
        name="out_mlp",
    )(*mix_args, w_o, x, g.reshape(1, D), w_up, w_down, g_final.reshape(1, D))


SWA_VROWS = SWA_HEAD_DIM + 16
SWA_STEP_BLOCKS = 4


def _swa_kernel(sink_ref, qt_ref, kc_ref, kp_ref, vtc_ref, vtp_ref, o_ref):
    n = pl.program_id(0)
    W, dh, NB = SWA_WINDOW, SWA_HEAD_DIM, SWA_STEP_BLOCKS
    G = SWA_HEADS // SWA_KV_HEADS
    key = lax.broadcasted_iota(jnp.int32, (2 * W, G * W), 0)
    lane = lax.broadcasted_iota(jnp.int32, (2 * W, G * W), 1)
    rel = lane % W + W - key
    in_window = (rel >= 0) & (rel < W)
    group_lane = lax.broadcasted_iota(jnp.int32, (1, G * W), 1) // W
    k_all = jnp.concatenate([kp_ref[...], kc_ref[...]], axis=0)
    vt_all = jnp.concatenate([vtp_ref[...], vtc_ref[...]], axis=1)
    ones_rows = jnp.ones((SWA_VROWS - dh, 2 * W), BF16)
    units = [(b, hk) for b in range(NB) for hk in range(SWA_KV_HEADS)]

    sts = []
    for b, hk in units:
        pair, half = divmod(hk, 2)
        lhs = k_all[b * W:(b + 2) * W, pair * LANES:(pair + 1) * LANES]
        q_cat = jnp.concatenate([qt_ref[(hk * G + g) * dh:(hk * G + g + 1) * dh, b * W:(b + 1) * W]
                                 for g in range(G)], axis=1)
        zeros = jnp.zeros_like(q_cat)
        rhs = jnp.concatenate([q_cat, zeros] if half == 0 else [zeros, q_cat], axis=0)
        sts.append(jnp.dot(lhs, rhs, preferred_element_type=F32))

    pts, sink_terms = [], []
    for (b, hk), st in zip(units, sts):
        visible = in_window if b > 0 else in_window & ((n > 0) | (key >= W))
        st = jnp.where(visible, st, NEG_BIG)
        sink = jnp.zeros((1, G * W), F32)
        for g in range(G):
            sink = jnp.where(group_lane == g, sink_ref[hk * G + g] * LOG2E, sink)
        m = jnp.maximum(jnp.max(st, axis=0, keepdims=True), sink)
        pts.append(jnp.exp2(st - m).astype(BF16))
        sink_terms.append(jnp.exp2(sink - m))

    accs = []
    for (b, hk), pt in zip(units, pts):
        vt_aug = jnp.concatenate([vt_all[hk * dh:(hk + 1) * dh, b * W:(b + 2) * W], ones_rows], axis=0)
        accs.append(jnp.dot(vt_aug, pt, preferred_element_type=F32))

    outs = [[] for _ in range(NB)]
    for (b, hk), acc, sink_term in zip(units, accs, sink_terms):
        ot = acc[:dh] / (acc[dh:dh + 1] + sink_term)
        for g in range(0, G, 2):
            two = jnp.concatenate([ot[:, g * W:(g + 1) * W], ot[:, (g + 1) * W:(g + 2) * W]], axis=0)
            outs[b].append(two.T)
    for b in range(NB):
        o_ref[b * W:(b + 1) * W, :] = jnp.concatenate(outs[b], axis=1).astype(o_ref.dtype)


def swa_attention(qt, k, vt, sinks):
    T = k.shape[0]
    W, NB = SWA_WINDOW, SWA_STEP_BLOCKS
    nq = SWA_HEADS * SWA_HEAD_DIM
    kvw = SWA_KV_HEADS * SWA_HEAD_DIM
    prev = lambda n: jnp.maximum(NB * n - 1, 0)
    return pl.pallas_call(
        _swa_kernel,
        grid=(T // (NB * W),),
        in_specs=[
            pl.BlockSpec(memory_space=pltpu.SMEM),
            pl.BlockSpec((nq, NB * W), lambda n: (0, n)),
            pl.BlockSpec((NB * W, kvw), lambda n: (n, 0)),
            pl.BlockSpec((W, kvw), lambda n: (prev(n), 0)),
            pl.BlockSpec((kvw, NB * W), lambda n: (0, n)),
            pl.BlockSpec((kvw, W), lambda n: (0, prev(n))),
        ],
        out_specs=pl.BlockSpec((NB * W, nq), lambda n: (n, 0)),
        out_shape=jax.ShapeDtypeStruct((T, nq), BF16),
        compiler_params=_cparams(("parallel",)),
        name="swa",
    )(sinks, qt, k, k, vt, vt)


def _split2(x):
    hi = x.astype(BF16)
    lo = (x - hi.astype(F32)).astype(BF16)
    return hi, lo


def _split_dot(m_bf16, parts):
    hi, lo = parts
    return (jnp.dot(m_bf16, hi, preferred_element_type=F32)
            + jnp.dot(m_bf16, lo, preferred_element_type=F32))


def _hgrn_kernel(q_ref, f_ref, v_ref, lb_ref, o_ref, st_ref):
    C, SUB, K, H = HGRN_CHUNK, HGRN_SUB, HGRN_EXPAND, HGRN_HEADS
    R = C * HGRN_STEP_CHUNKS
    nt = (((1,), (1,)), ((), ()))
    tn = (((0,), (0,)), ((), ()))

    @pl.when(pl.program_id(0) == 0)
    def _():
        st_ref[...] = jnp.zeros_like(st_ref)

    row = lax.broadcasted_iota(jnp.int32, (R, R), 0)
    col = lax.broadcasted_iota(jnp.int32, (R, R), 1)
    same_chunk = row // C == col // C
    heads = [slice(h * K, (h + 1) * K) for h in range(H)]
    chunks = [slice(c * C, (c + 1) * C) for c in range(HGRN_STEP_CHUNKS)]

    lb = lb_ref[...]
    f = lb + (1.0 - lb) * jax.nn.sigmoid(f_ref[...])
    logf = _split2(jnp.log2(f))
    kk = 1.0 - f
    q = q_ref[...]
    qs = q * jax.nn.sigmoid(q)
    v = v_ref[...]
    bcum = _split_dot(((row >= col) & same_chunk).astype(BF16), logf)
    to_end = _split_dot(((row < col) & same_chunk).astype(BF16), logf)

    q_in = (qs * jnp.exp2(bcum)).astype(BF16)
    k_out = (kk * jnp.exp2(to_end)).astype(BF16)
    o_in = [[] for _ in heads]
    for cs in chunks:
        decay = jnp.exp2(bcum[cs.stop - 1:cs.stop, :])
        for h, hs in enumerate(heads):
            st = st_ref[h]
            o_in[h].append(lax.dot_general(q_in[cs, hs], st.astype(BF16), nt, preferred_element_type=F32))
            upd = lax.dot_general(v[cs, hs], k_out[cs, hs], tn, preferred_element_type=F32)
            st_ref[h] = st * decay[:, hs] + upd
    o = [jnp.concatenate(parts, axis=0) for parts in o_in]


    a = [jnp.zeros((R, R), F32) for _ in heads]
    half = SUB
    while half < C:
        pos = row % (2 * half)
        bnd = row - pos + half - 1
        to_bnd = (((pos >= half) & (col > bnd) & (col <= row))
                  | ((pos < half) & (col > row) & (col <= bnd)))
        e = jnp.exp2(_split_dot(to_bnd.astype(BF16), logf))
        q_e = (qs * e).astype(BF16)
        k_e = (kk * e).astype(BF16)
        right_left = (pos >= half) & (col <= bnd) & (col > bnd - half)
        for h, hs in enumerate(heads):
            a_l = lax.dot_general(q_e[:, hs], k_e[:, hs], nt, preferred_element_type=F32)
            a[h] = jnp.where(right_left, a_l, a[h])
        half *= 2

    sub_pos = lax.broadcasted_iota(jnp.int32, (R, H * K), 0) % SUB
    for d in range(SUB):
        if d == 0:
            e = qs * kk
        else:
            e = qs * pltpu.roll(kk, d, 0) * jnp.exp2(bcum - pltpu.roll(bcum, d, 0))
            e = jnp.where(sub_pos >= d, e, 0.0)
        band = (row - col == d) & (row % SUB >= d)
        for h, hs in enumerate(heads):
            a[h] = jnp.where(band, jnp.sum(e[:, hs], axis=1, keepdims=True), a[h])

    for h, hs in enumerate(heads):
        o[h] = o[h] + jnp.dot(a[h].astype(BF16), v[:, hs], preferred_element_type=F32)
    o_ref[...] = jnp.concatenate(o, axis=1)


def hgrn_recurrence(q, f_logit, v, lb):
    T = q.shape[0]
    rows = HGRN_CHUNK * HGRN_STEP_CHUNKS
    width = HGRN_HEADS * HGRN_EXPAND
    blk = pl.BlockSpec((rows, width), lambda c: (c, 0))
    return pl.pallas_call(
        _hgrn_kernel,
        grid=(T // rows,),
        in_specs=[blk, blk, blk, _const_spec((1, width))],
        out_specs=blk,
        out_shape=jax.ShapeDtypeStruct((T, width), F32),
        scratch_shapes=[pltpu.VMEM((HGRN_HEADS, HGRN_HEAD_DIM, HGRN_EXPAND), F32)],
        compiler_params=_cparams(("arbitrary",)),
        name="hgrn",
    )(q, f_logit, v, lb.reshape(1, -1))


FOX_CUM_TILE = 256


def _split3(x):
    hi = x.astype(BF16)
    r1 = x - hi.astype(F32)
    mid = r1.astype(BF16)
    lo = (r1 - mid.astype(F32)).astype(BF16)
    return hi, mid, lo


FOX_ONES_ROW = 3 * FOX_HEADS


def _fox_cumsum_kernel(f_ref, kaug_ref, ct_ref, carry_ref):
    @pl.when(pl.program_id(0) == 0)
    def _():
        carry_ref[...] = jnp.zeros_like(carry_ref)

    x = f_ref[...]
    ls = jnp.minimum(x, 0.0) - jnp.log(1.0 + jnp.exp(-jnp.abs(x)))
    n = FOX_CUM_TILE
    tril = (lax.broadcasted_iota(jnp.int32, (n, n), 0) >= lax.broadcasted_iota(jnp.int32, (n, n), 1)).astype(BF16)
    hi, mid, lo = _split3(ls)
    cum = (jnp.dot(tril, hi, preferred_element_type=F32)
           + jnp.dot(tril, mid, preferred_element_type=F32)
           + jnp.dot(tril, lo, preferred_element_type=F32))
    c = cum + carry_ref[...]
    carry_ref[...] = c[n - 1:n, :]
    H = FOX_HEADS
    hi2, mid2, lo2 = (a.astype(F32) for a in _split3(c * LOG2E))
    lane = lax.broadcasted_iota(jnp.int32, (n, LANES), 1)
    packed = jnp.where(lane < H, hi2,
                       jnp.where(lane < 2 * H, pltpu.roll(mid2, H, 1),
                                 jnp.where(lane < 3 * H, pltpu.roll(lo2, 2 * H, 1), 0.0)))
    is_one = (lane >= FOX_ONES_ROW) & (lane < FOX_ONES_ROW + 3)
    kaug_ref[...] = jnp.where(is_one, 1.0, -packed).astype(BF16)
    ct_ref[...] = packed.T


def fox_cumsum(f_logit):
    T = f_logit.shape[0]
    n = FOX_CUM_TILE
    spec = pl.BlockSpec((n, LANES), lambda i: (i, 0))
    return pl.pallas_call(
        _fox_cumsum_kernel,
        grid=(T // n,),
        in_specs=[spec],
        out_specs=[spec, pl.BlockSpec((LANES, n), lambda i: (0, i))],
        out_shape=[jax.ShapeDtypeStruct((T, LANES), BF16), jax.ShapeDtypeStruct((LANES, T), F32)],
        scratch_shapes=[pltpu.VMEM((1, LANES), F32)],
        compiler_params=_cparams(("arbitrary",)),
        name="fox_cumsum",
    )(f_logit)


FOX_TILE = 512
FOX_KEY_TILE = 256
FOX_STEP_HEADS = 2


FOX_VROWS = FOX_HEAD_DIM + 16


def _fox_kernel(first_ref, k_ref, kaug_ref, qt_ref, ct_ref, vt_ref, o_ref,
                rhs_ref, acc_ref, m_ref, alpha_ref, st_ref, pt_ref):
    p = pl.program_id(0)
    i = pl.program_id(1)
    tq, tk = FOX_TILE, FOX_KEY_TILE
    dh = FOX_HEAD_DIM
    H = FOX_HEADS

    NH = FOX_STEP_HEADS
    row = lax.broadcasted_iota(jnp.int32, (LANES, tq), 0)
    for h in range(NH):
        head = NH * p + h
        pair, half = divmod(h, 2)
        qt = qt_ref[pair * LANES:(pair + 1) * LANES, :].astype(F32)
        q_part = jnp.where((row >= half * dh) & (row < (half + 1) * dh), qt, 0.0)
        gate_rows = (row == head) | (row == H + head) | (row == 2 * H + head)
        c_parts = [ct_ref[pl.ds(part * H + head, 1), :] for part in range(3)]
        aug = jnp.where(gate_rows, 1.0, 0.0)
        for part in range(3):
            aug = jnp.where(row == FOX_ONES_ROW + part, c_parts[part], aug)
        rhs_ref[h] = jnp.concatenate([q_part, aug], axis=0).astype(BF16)

    acc_ref[...] = jnp.zeros_like(acc_ref)
    m_ref[...] = jnp.full_like(m_ref, NEG_BIG)
    pt_ref[1] = jnp.zeros(pt_ref.shape[1:], BF16)
    alpha_ref[1] = jnp.ones(alpha_ref.shape[1:], F32)

    def logits(j, slot):
        start = pl.multiple_of(j * tk, tk)
        gates = kaug_ref[pl.ds(start, tk), :]
        for h in range(NH):
            pair = h // 2
            lhs = jnp.concatenate([k_ref[pl.ds(start, tk), pair * LANES:(pair + 1) * LANES], gates], axis=1)
            st_ref[slot, h] = jnp.dot(lhs, rhs_ref[h], preferred_element_type=F32)

    def softmax(slot, mask_offset=None):
        for h in range(NH):
            st = st_ref[slot, h]
            if mask_offset is not None:
                causal = (lax.broadcasted_iota(jnp.int32, (tk, tq), 0) + mask_offset
                          <= lax.broadcasted_iota(jnp.int32, (tk, tq), 1))
                st = jnp.where(causal, st, NEG_BIG)
            m_old = m_ref[h]
            m_new = jnp.maximum(m_old, jnp.max(st, axis=0, keepdims=True))
            alpha_ref[slot, h] = jnp.exp2(m_old - m_new)
            pt_ref[slot, h] = jnp.exp2(st - m_new).astype(BF16)
            m_ref[h] = m_new

    ones_rows = jnp.ones((FOX_VROWS - dh, tk), BF16)

    def accum(j, slot):
        start = pl.multiple_of(j * tk, tk)
        for h in range(NH):
            vt_aug = jnp.concatenate([vt_ref[h * dh:(h + 1) * dh, pl.ds(start, tk)], ones_rows], axis=0)
            pv = jnp.dot(vt_aug, pt_ref[slot, h], preferred_element_type=F32)
            acc_ref[h] = alpha_ref[slot, h] * acc_ref[h] + pv

    sub = tq // tk

    def body(p, carry):
        accum(jnp.maximum(sub * p - 1, 0), 1)
        softmax(0)
        logits(sub * p + 1, 1)
        accum(sub * p, 0)
        softmax(1)
        logits(sub * p + 2, 0)
        return carry

    first = first_ref[p, i]
    logits(sub * first, 0)
    lax.fori_loop(first, i, body, 0)
    logits(sub * i + 1, 1)
    accum(jnp.maximum(sub * i - 1, 0), 1)
    softmax(0, mask_offset=0)
    accum(sub * i, 0)
    softmax(1, mask_offset=tk)
    accum(sub * i + 1, 1)

    for pair in range(NH // 2):
        acc_a, acc_b = acc_ref[2 * pair], acc_ref[2 * pair + 1]
        ot = jnp.concatenate([acc_a[:dh] / acc_a[dh:dh + 1], acc_b[:dh] / acc_b[dh:dh + 1]], axis=0)
        o_ref[:, pair * LANES:(pair + 1) * LANES] = ot.T.astype(o_ref.dtype)


def _fox_norms_kernel(k_ref, qt_ref, kn_ref, qn_ref):
    n = FOX_HEADS * FOX_HEAD_DIM
    chan = lax.broadcasted_iota(jnp.int32, (n, LANES), 0) // FOX_HEAD_DIM
    head = lax.broadcasted_iota(jnp.int32, (n, LANES), 1)
    group = (chan == head).astype(BF16)
    k = k_ref[...].astype(F32)
    k2 = jnp.dot((k * k).astype(BF16), group, preferred_element_type=F32)
    q = qt_ref[...].astype(F32)
    q2 = lax.dot_general((q * q).astype(BF16), group, (((0,), (0,)), ((), ())), preferred_element_type=F32)
    kn_ref[0] = jnp.broadcast_to(jnp.max(k2, axis=0, keepdims=True), kn_ref.shape[1:])
    qn_ref[0] = jnp.broadcast_to(jnp.max(q2, axis=0, keepdims=True), qn_ref.shape[1:])


def fox_norms(k, qt):
    T = k.shape[0]
    t = FOX_TILE
    n = FOX_HEADS * FOX_HEAD_DIM
    out = pl.BlockSpec((1, 8, LANES), lambda i: (i, 0, 0))
    return pl.pallas_call(
        _fox_norms_kernel,
        grid=(T // t,),
        in_specs=[pl.BlockSpec((t, n), lambda i: (i, 0)), pl.BlockSpec((n, t), lambda i: (0, i))],
        out_specs=[out, out],
        out_shape=[jax.ShapeDtypeStruct((T // t, 8, LANES), F32)] * 2,
        compiler_params=_cparams(("parallel",)),
        name="fox_norms",
    )(k, qt)


FOX_UNDERFLOW_LOG2 = 160.0


def fox_first_block(kn2, qn2, ct):
    H = FOX_HEADS
    nt = kn2.shape[0]
    bk = jnp.sqrt(jnp.max(kn2[:, 0, :H], axis=0))
    bq = jnp.sqrt(qn2[:, 0, :H])
    need = 2.0 * 1.02 * bq * bk[None, :] + FOX_UNDERFLOW_LOG2 + 2.0
    c_first = ct[:, 0::FOX_TILE]
    c_last = ct[:, FOX_TILE - 1::FOX_TILE]
    c_first = c_first[:H] + c_first[H:2 * H] + c_first[2 * H:3 * H]
    c_last = c_last[:H] + c_last[H:2 * H] + c_last[2 * H:3 * H]
    gap = c_last[:, None, :] - c_first[:, :, None]
    earlier = jnp.arange(nt)[None, :] < jnp.arange(nt)[:, None]
    prunable = (gap > need.T[:, :, None]) & earlier[None]
    n_skip = jnp.min(jnp.where(prunable, nt, jnp.arange(nt)[None, None, :]), axis=2)
    return jnp.min(n_skip.reshape(H // FOX_STEP_HEADS, FOX_STEP_HEADS, nt), axis=1).astype(jnp.int32)


def fox_attention(first, k, kaug, qt, ct, vt):
    T = k.shape[0]
    t = FOX_TILE
    nh = FOX_STEP_HEADS
    width = nh * FOX_HEAD_DIM
    return pl.pallas_call(
        _fox_kernel,
        grid=(FOX_HEADS // nh, T // t),
        in_specs=[
            pl.BlockSpec(memory_space=pltpu.SMEM),
            pl.BlockSpec((T, width), lambda p, i: (0, p)),
            pl.BlockSpec((T, LANES), lambda p, i: (0, 0), pipeline_mode=pl.Buffered(1)),
            pl.BlockSpec((width, t), lambda p, i: (p, i)),
            pl.BlockSpec((LANES, t), lambda p, i: (0, i)),
            pl.BlockSpec((width, T), lambda p, i: (p, 0)),
        ],
        out_specs=pl.BlockSpec((t, width), lambda p, i: (i, p)),
        out_shape=jax.ShapeDtypeStruct((T, FOX_HEADS * FOX_HEAD_DIM), BF16),
        scratch_shapes=[
            pltpu.VMEM((nh, 2 * LANES, t), BF16),
            pltpu.VMEM((nh, FOX_VROWS, t), F32),
            pltpu.VMEM((nh, 1, t), F32),
            pltpu.VMEM((2, nh, 1, t), F32),
            pltpu.VMEM((2, nh, FOX_KEY_TILE, t), F32),
            pltpu.VMEM((2, nh, FOX_KEY_TILE, t), BF16),
        ],
        compiler_params=_cparams(("parallel", "arbitrary")),
        name="fox",
    )(first, k, kaug, qt, ct, vt)


def kernel(x, norm_mix, norm_mlp, norm_final, w_up, w_down, swa_w_qkv, swa_b_qkv, swa_sinks, swa_w_o,
           hgrn_w_in, hgrn_lb_logits, hgrn_g_norm, hgrn_w_o, fox_w_in, fox_b_in, fox_w_o):
    B, T, D = x.shape
    depth = norm_mix.shape[0]
    lb_soft = jax.nn.softmax(hgrn_lb_logits.astype(F32), axis=0)
    lower_bounds = jnp.cumsum(lb_soft, axis=0) - lb_soft[0]

    outs = []
    for b in range(B):
        xb = x[b]
        for i in range(depth):
            m, j = i % N_MIXERS, i // N_MIXERS
            if m == 0:
                nq = SWA_HEADS * SWA_HEAD_DIM
                nkv = SWA_KV_HEADS * SWA_HEAD_DIM
                col_scale = jnp.concatenate([jnp.full((nq,), LOG2E * SWA_HEAD_DIM ** -0.5, F32),
                                             jnp.ones((2 * nkv,), F32)])
                qt, k, vt = norm_proj(xb, norm_mix[i], (swa_w_qkv[j] * col_scale).astype(BF16),
                                      swa_b_qkv[j] * col_scale,
                                      [(0, nq), (nq, nq + nkv), (nq + nkv, nq + 2 * nkv)], [BF16] * 3, "swa_proj",
                                      transposed=(True, False, True))
                a = swa_attention(qt, k, vt, swa_sinks[j])
                w_o = swa_w_o[j]
            elif m == 1:
                hk = HGRN_HEADS * HGRN_EXPAND
                hv = HGRN_HEADS * HGRN_HEAD_DIM
                q, f, v, g = norm_proj(xb, norm_mix[i], hgrn_w_in[j].astype(BF16), jnp.zeros((2 * hk + 2 * hv,), F32),
                                       [(0, hk), (hk, 2 * hk), (2 * hk, 2 * hk + hv), (2 * hk + hv, 2 * hk + 2 * hv)],
                                       [F32, F32, BF16, F32], "hgrn_proj")
                o = hgrn_recurrence(q, f, v, lower_bounds[i])
                a = (o, g, hgrn_g_norm[j])
                w_o = hgrn_w_o[j]
            else:
                n = FOX_HEADS * FOX_HEAD_DIM
                pad = LANES - FOX_HEADS
                col_scale = jnp.concatenate([jnp.full((n,), LOG2E * FOX_HEAD_DIM ** -0.5, F32),
                                             jnp.ones((2 * n + FOX_HEADS,), F32)])
                w = jnp.pad(fox_w_in[j] * col_scale, ((0, 0), (0, pad))).astype(BF16)
                bias = jnp.pad(fox_b_in[j] * col_scale, ((0, pad),))
                qt, k, vt, f = norm_proj(xb, norm_mix[i], w, bias,
                                         [(0, n), (n, 2 * n), (2 * n, 3 * n), (3 * n, 3 * n + LANES)],
                                         [BF16, BF16, BF16, F32], "fox_proj",
                                         transposed=(True, False, True, False))
                kaug, ct = fox_cumsum(f)
                first = fox_first_block(*fox_norms(k, qt), ct)
                a = fox_attention(first, k, kaug, qt, ct, vt)
                w_o = fox_w_o[j]
            last = i == depth - 1
            xb = out_mlp(a, w_o.astype(BF16), xb, norm_mlp[i], w_up[i].astype(BF16), w_down[i].astype(BF16),
                         norm_final, last)
        outs.append(xb)
    return jnp.stack(outs, axis=0)
```

```python
import functools

import jax
import jax.numpy as jnp
from jax import lax
from jax.experimental import pallas as pl
from jax.experimental.pallas import tpu as pltpu

F32 = jnp.float32
BF16 = jnp.bfloat16

D_MODEL = 1024
D_FF = 4 * D_MODEL
EPS = 1e-6
N_MIXERS = 3

SWA_HEADS = 16
SWA_KV_HEADS = 4
SWA_HEAD_DIM = 64
SWA_WINDOW = 128

HGRN_HEADS = 8
HGRN_EXPAND = 128
HGRN_HEAD_DIM = 128
HGRN_CHUNK = 64
HGRN_SUB = 8
HGRN_STEP_CHUNKS = 2

FOX_HEADS = 16
FOX_HEAD_DIM = 64

LANES = 128
VMEM_LIMIT = 56 * 1024 * 1024
NEG_BIG = -1e30
LOG2E = 1.4426950408889634

ROW_TILE = 1024


def _cparams(sem):
    return pltpu.CompilerParams(dimension_semantics=sem, vmem_limit_bytes=VMEM_LIMIT)


def _const_spec(shape):
    nd = len(shape)
    return pl.BlockSpec(shape, lambda *_: (0,) * nd, pipeline_mode=pl.Buffered(1))


def _layer_spec(shape, layer):
    nd = len(shape)
    return pl.BlockSpec((None,) + tuple(shape), lambda *_: (layer,) + (0,) * nd, pipeline_mode=pl.Buffered(1))


def _rms(x, g):
    ms = jnp.mean(x * x, axis=-1, keepdims=True)
    return x * lax.rsqrt(ms + EPS) * g


def _norm_proj_kernel(x_ref, g_ref, w_ref, b_ref, *o_refs, segments, transposed, n_chunk):
    h = _rms(x_ref[...], g_ref[...]).astype(BF16)
    for o_ref, (lo, hi), tr in zip(o_refs, segments, transposed):
        for j in range(lo, hi, n_chunk):
            je = min(j + n_chunk, hi)
            y = jnp.dot(h, w_ref[:, j:je], preferred_element_type=F32) + b_ref[:, j:je]
            if tr:
                o_ref[j - lo:je - lo, :] = y.T.astype(o_ref.dtype)
            else:
                o_ref[:, j - lo:je - lo] = y.astype(o_ref.dtype)


def norm_proj(x, g, w, b, segments, dtypes, name, transposed=None):
    T, D = x.shape
    N = w.shape[1]
    tm = ROW_TILE
    transposed = tuple(transposed) if transposed is not None else (False,) * len(segments)
    out_shape, out_specs = [], []
    for (lo, hi), dt, tr in zip(segments, dtypes, transposed):
        if tr:
            out_shape.append(jax.ShapeDtypeStruct((hi - lo, T), dt))
            out_specs.append(pl.BlockSpec((hi - lo, tm), lambda i: (0, i)))
        else:
            out_shape.append(jax.ShapeDtypeStruct((T, hi - lo), dt))
            out_specs.append(pl.BlockSpec((tm, hi - lo), lambda i: (i, 0)))
    return pl.pallas_call(
        functools.partial(_norm_proj_kernel, segments=tuple(segments), transposed=transposed, n_chunk=512),
        grid=(T // tm,),
        in_specs=[
            pl.BlockSpec((tm, D), lambda i: (i, 0)),
            _const_spec((1, D)),
            _const_spec((D, N)),
            _const_spec((1, N)),
        ],
        out_specs=out_specs,
        out_shape=out_shape,
        compiler_params=_cparams(("parallel",)),
        name=name,
    )(x, g.reshape(1, D), w, b.reshape(1, N))


def _hgrn_gate(r_ref, gate_ref, gain_ref):
    parts = []
    for h in range(HGRN_HEADS):
        sl = slice(h * HGRN_HEAD_DIM, (h + 1) * HGRN_HEAD_DIM)
        r = r_ref[:, sl]
        r = r * lax.rsqrt(jnp.mean(r * r, axis=-1, keepdims=True) + EPS)
        g = gate_ref[:, sl]
        parts.append((r * gain_ref[:, sl] * (g * jax.nn.sigmoid(g))).astype(BF16))
    return jnp.concatenate(parts, axis=1)


def _mlp_tail(x, g_ref, wu_ref, wd_ref, gf_ref, o_ref, ff_chunk, final_norm):
    h = _rms(x, g_ref[...]).astype(BF16)
    acc = x
    for j in range(0, D_FF, ff_chunk):
        u = jnp.dot(h, wu_ref[:, j:j + ff_chunk], preferred_element_type=F32)
        u = jnp.maximum(u, 0.0)
        u = (u * u).astype(BF16)
        acc = acc + jnp.dot(u, wd_ref[j:j + ff_chunk, :], preferred_element_type=F32)
    if final_norm:
        acc = _rms(acc, gf_ref[...])
    o_ref[...] = acc


def _out_mlp_kernel(a_ref, wo_ref, x_ref, g_ref, wu_ref, wd_ref, gf_ref, o_ref, *, ff_chunk, final_norm):
    x = x_ref[...] + jnp.dot(a_ref[...], wo_ref[...], preferred_element_type=F32)
    _mlp_tail(x, g_ref, wu_ref, wd_ref, gf_ref, o_ref, ff_chunk, final_norm)


def _gated_out_mlp_kernel(r_ref, gate_ref, gain_ref, wo_ref, x_ref, g_ref, wu_ref, wd_ref, gf_ref, o_ref, *,
                          ff_chunk, final_norm):
    a = _hgrn_gate(r_ref, gate_ref, gain_ref)
    x = x_ref[...] + jnp.dot(a, wo_ref[...], preferred_element_type=F32)
    _mlp_tail(x, g_ref, wu_ref, wd_ref, gf_ref, o_ref, ff_chunk, final_norm)


def out_mlp(mixer_out, w_o, x, g, w_up_all, w_down_all, layer, g_final, final_norm):
    T, D = x.shape
    gated = isinstance(mixer_out, tuple)
    tm = ROW_TILE // 2 if gated else ROW_TILE
    row = pl.BlockSpec((tm, D), lambda i: (i, 0))
    if gated:
        r, gate, gain = mixer_out
        body, mix_args, mix_specs = _gated_out_mlp_kernel, (r, gate, gain.reshape(1, D)), [row, row, _const_spec((1, D))]
    else:
        body, mix_args, mix_specs = _out_mlp_kernel, (mixer_out,), [row]
    return pl.pallas_call(
        functools.partial(body, ff_chunk=512, final_norm=final_norm),
        grid=(T // tm,),
        in_specs=mix_specs + [
            _const_spec((D, D)),
            row,
            _const_spec((1, D)),
            _layer_spec((D, D_FF), layer),
            _layer_spec((D_FF, D), layer),
            _const_spec((1, D)),
        ],
        out_specs=row,
        out_shape=jax.ShapeDtypeStruct((T, D), F32),
        compiler_params=_cparams(("parallel",)),
        name="out_mlp",
    )(*mix_args, w_o, x, g.reshape(1, D), w_up_all, w_down_all, g_final.reshape(1, D))


SWA_VROWS = SWA_HEAD_DIM + 16
SWA_STEP_BLOCKS = 4


def _swa_kernel(sink_ref, qt_ref, kc_ref, kp_ref, vtc_ref, vtp_ref, o_ref):
    n = pl.program_id(0)
    W, dh, NB = SWA_WINDOW, SWA_HEAD_DIM, SWA_STEP_BLOCKS
    G = SWA_HEADS // SWA_KV_HEADS
    key = lax.broadcasted_iota(jnp.int32, (2 * W, G * W), 0)
    lane = lax.broadcasted_iota(jnp.int32, (2 * W, G * W), 1)
    rel = lane % W + W - key
    in_window = (rel >= 0) & (rel < W)
    group_lane = lax.broadcasted_iota(jnp.int32, (1, G * W), 1) // W
    k_all = jnp.concatenate([kp_ref[...], kc_ref[...]], axis=0)
    vt_all = jnp.concatenate([vtp_ref[...], vtc_ref[...]], axis=1)
    ones_rows = jnp.ones((SWA_VROWS - dh, 2 * W), BF16)
    units = [(b, hk) for b in range(NB) for hk in range(SWA_KV_HEADS)]

    sts = []
    for b, hk in units:
        pair, half = divmod(hk, 2)
        lhs = k_all[b * W:(b + 2) * W, pair * LANES:(pair + 1) * LANES]
        q_cat = jnp.concatenate([qt_ref[(hk * G + g) * dh:(hk * G + g + 1) * dh, b * W:(b + 1) * W]
                                 for g in range(G)], axis=1)
        zeros = jnp.zeros_like(q_cat)
        rhs = jnp.concatenate([q_cat, zeros] if half == 0 else [zeros, q_cat], axis=0)
        sts.append(jnp.dot(lhs, rhs, preferred_element_type=F32))

    pts, sink_terms = [], []
    for (b, hk), st in zip(units, sts):
        visible = in_window if b > 0 else in_window & ((n > 0) | (key >= W))
        st = jnp.where(visible, st, NEG_BIG)
        sink = jnp.zeros((1, G * W), F32)
        for g in range(G):
            sink = jnp.where(group_lane == g, sink_ref[hk * G + g] * LOG2E, sink)
        m = jnp.maximum(jnp.max(st, axis=0, keepdims=True), sink)
        pts.append(jnp.exp2(st - m).astype(BF16))
        sink_terms.append(jnp.exp2(sink - m))

    accs = []
    for (b, hk), pt in zip(units, pts):
        vt_aug = jnp.concatenate([vt_all[hk * dh:(hk + 1) * dh, b * W:(b + 2) * W], ones_rows], axis=0)
        accs.append(jnp.dot(vt_aug, pt, preferred_element_type=F32))

    outs = [[] for _ in range(NB)]
    for (b, hk), acc, sink_term in zip(units, accs, sink_terms):
        ot = acc[:dh] / (acc[dh:dh + 1] + sink_term)
        for g in range(0, G, 2):
            two = jnp.concatenate([ot[:, g * W:(g + 1) * W], ot[:, (g + 1) * W:(g + 2) * W]], axis=0)
            outs[b].append(two.T)
    for b in range(NB):
        o_ref[b * W:(b + 1) * W, :] = jnp.concatenate(outs[b], axis=1).astype(o_ref.dtype)


def swa_attention(qt, k, vt, sinks):
    T = k.shape[0]
    W, NB = SWA_WINDOW, SWA_STEP_BLOCKS
    nq = SWA_HEADS * SWA_HEAD_DIM
    kvw = SWA_KV_HEADS * SWA_HEAD_DIM
    prev = lambda n: jnp.maximum(NB * n - 1, 0)
    return pl.pallas_call(
        _swa_kernel,
        grid=(T // (NB * W),),
        in_specs=[
            pl.BlockSpec(memory_space=pltpu.SMEM),
            pl.BlockSpec((nq, NB * W), lambda n: (0, n)),
            pl.BlockSpec((NB * W, kvw), lambda n: (n, 0)),
            pl.BlockSpec((W, kvw), lambda n: (prev(n), 0)),
            pl.BlockSpec((kvw, NB * W), lambda n: (0, n)),
            pl.BlockSpec((kvw, W), lambda n: (0, prev(n))),
        ],
        out_specs=pl.BlockSpec((NB * W, nq), lambda n: (n, 0)),
        out_shape=jax.ShapeDtypeStruct((T, nq), BF16),
        compiler_params=_cparams(("parallel",)),
        name="swa",
    )(sinks, qt, k, k, vt, vt)


def _split2(x):
    hi = x.astype(BF16)
    lo = (x - hi.astype(F32)).astype(BF16)
    return hi, lo


def _split_dot(m_bf16, parts):
    hi, lo = parts
    return (jnp.dot(m_bf16, hi, preferred_element_type=F32)
            + jnp.dot(m_bf16, lo, preferred_element_type=F32))


def _hgrn_kernel(q_ref, f_ref, v_ref, lb_ref, o_ref, st_ref):
    C, SUB, K, H = HGRN_CHUNK, HGRN_SUB, HGRN_EXPAND, HGRN_HEADS
    R = C * HGRN_STEP_CHUNKS
    nt = (((1,), (1,)), ((), ()))
    tn = (((0,), (0,)), ((), ()))

    @pl.when(pl.program_id(0) == 0)
    def _():
        st_ref[...] = jnp.zeros_like(st_ref)

    row = lax.broadcasted_iota(jnp.int32, (R, R), 0)
    col = lax.broadcasted_iota(jnp.int32, (R, R), 1)
    same_chunk = row // C == col // C
    heads = [slice(h * K, (h + 1) * K) for h in range(H)]
    chunks = [slice(c * C, (c + 1) * C) for c in range(HGRN_STEP_CHUNKS)]

    lb = lb_ref[...]
    f = lb + (1.0 - lb) * jax.nn.sigmoid(f_ref[...])
    logf = _split2(jnp.log2(f))
    kk = 1.0 - f
    q = q_ref[...]
    qs = q * jax.nn.sigmoid(q)
    v = v_ref[...]
    bcum = _split_dot(((row >= col) & same_chunk).astype(BF16), logf)
    to_end = _split_dot(((row < col) & same_chunk).astype(BF16), logf)

    q_in = (qs * jnp.exp2(bcum)).astype(BF16)
    k_out = (kk * jnp.exp2(to_end)).astype(BF16)
    o_in = [[] for _ in heads]
    for cs in chunks:
        decay = jnp.exp2(bcum[cs.stop - 1:cs.stop, :])
        for h, hs in enumerate(heads):
            st = st_ref[h]
            o_in[h].append(lax.dot_general(q_in[cs, hs], st.astype(BF16), nt, preferred_element_type=F32))
            upd = lax.dot_general(v[cs, hs], k_out[cs, hs], tn, preferred_element_type=F32)
            st_ref[h] = st * decay[:, hs] + upd
    o = [jnp.concatenate(parts, axis=0) for parts in o_in]


    a = [jnp.zeros((R, R), F32) for _ in heads]
    half = SUB
    while half < C:
        pos = row % (2 * half)
        bnd = row - pos + half - 1
        to_bnd = (((pos >= half) & (col > bnd) & (col <= row))
                  | ((pos < half) & (col > row) & (col <= bnd)))
        e = jnp.exp2(_split_dot(to_bnd.astype(BF16), logf))
        q_e = (qs * e).astype(BF16)
        k_e = (kk * e).astype(BF16)
        right_left = (pos >= half) & (col <= bnd) & (col > bnd - half)
        for h, hs in enumerate(heads):
            a_l = lax.dot_general(q_e[:, hs], k_e[:, hs], nt, preferred_element_type=F32)
            a[h] = jnp.where(right_left, a_l, a[h])
        half *= 2

    sub_pos = lax.broadcasted_iota(jnp.int32, (R, H * K), 0) % SUB
    for d in range(SUB):
        if d == 0:
            e = qs * kk
        else:
            e = qs * pltpu.roll(kk, d, 0) * jnp.exp2(bcum - pltpu.roll(bcum, d, 0))
            e = jnp.where(sub_pos >= d, e, 0.0)
        band = (row - col == d) & (row % SUB >= d)
        for h, hs in enumerate(heads):
            a[h] = jnp.where(band, jnp.sum(e[:, hs], axis=1, keepdims=True), a[h])

    for h, hs in enumerate(heads):
        o[h] = o[h] + jnp.dot(a[h].astype(BF16), v[:, hs], preferred_element_type=F32)
    o_ref[...] = jnp.concatenate(o, axis=1)


def hgrn_recurrence(q, f_logit, v, lb):
    T = q.shape[0]
    rows = HGRN_CHUNK * HGRN_STEP_CHUNKS
    width = HGRN_HEADS * HGRN_EXPAND
    blk = pl.BlockSpec((rows, width), lambda c: (c, 0))
    return pl.pallas_call(
        _hgrn_kernel,
        grid=(T // rows,),
        in_specs=[blk, blk, blk, _const_spec((1, width))],
        out_specs=blk,
        out_shape=jax.ShapeDtypeStruct((T, width), F32),
        scratch_shapes=[pltpu.VMEM((HGRN_HEADS, HGRN_HEAD_DIM, HGRN_EXPAND), F32)],
        compiler_params=_cparams(("arbitrary",)),
        name="hgrn",
    )(q, f_logit, v, lb.reshape(1, -1))


FOX_CUM_TILE = 256


def _split3(x):
    hi = x.astype(BF16)
    r1 = x - hi.astype(F32)
    mid = r1.astype(BF16)
    lo = (r1 - mid.astype(F32)).astype(BF16)
    return hi, mid, lo


FOX_ONES_ROW = 3 * FOX_HEADS


def _fox_cumsum_kernel(f_ref, kaug_ref, ct_ref, carry_ref):
    @pl.when(pl.program_id(0) == 0)
    def _():
        carry_ref[...] = jnp.zeros_like(carry_ref)

    x = f_ref[...]
    ls = jnp.minimum(x, 0.0) - jnp.log(1.0 + jnp.exp(-jnp.abs(x)))
    n = FOX_CUM_TILE
    tril = (lax.broadcasted_iota(jnp.int32, (n, n), 0) >= lax.broadcasted_iota(jnp.int32, (n, n), 1)).astype(BF16)
    hi, mid, lo = _split3(ls)
    cum = (jnp.dot(tril, hi, preferred_element_type=F32)
           + jnp.dot(tril, mid, preferred_element_type=F32)
           + jnp.dot(tril, lo, preferred_element_type=F32))
    c = cum + carry_ref[...]
    carry_ref[...] = c[n - 1:n, :]
    H = FOX_HEADS
    hi2, mid2, lo2 = (a.astype(F32) for a in _split3(c * LOG2E))
    lane = lax.broadcasted_iota(jnp.int32, (n, LANES), 1)
    packed = jnp.where(lane < H, hi2,
                       jnp.where(lane < 2 * H, pltpu.roll(mid2, H, 1),
                                 jnp.where(lane < 3 * H, pltpu.roll(lo2, 2 * H, 1), 0.0)))
    is_one = (lane >= FOX_ONES_ROW) & (lane < FOX_ONES_ROW + 3)
    kaug_ref[...] = jnp.where(is_one, 1.0, -packed).astype(BF16)
    ct_ref[...] = packed.T


def fox_cumsum(f_logit):
    T = f_logit.shape[0]
    n = FOX_CUM_TILE
    spec = pl.BlockSpec((n, LANES), lambda i: (i, 0))
    return pl.pallas_call(
        _fox_cumsum_kernel,
        grid=(T // n,),
        in_specs=[spec],
        out_specs=[spec, pl.BlockSpec((LANES, n), lambda i: (0, i))],
        out_shape=[jax.ShapeDtypeStruct((T, LANES), BF16), jax.ShapeDtypeStruct((LANES, T), F32)],
        scratch_shapes=[pltpu.VMEM((1, LANES), F32)],
        compiler_params=_cparams(("arbitrary",)),
        name="fox_cumsum",
    )(f_logit)


FOX_TILE = 512
FOX_KEY_TILE = 256
FOX_STEP_HEADS = 2


FOX_VROWS = FOX_HEAD_DIM + 16


def _fox_kernel(first_ref, k_ref, kaug_ref, qt_ref, ct_ref, vt_ref, o_ref,
                rhs_ref, acc_ref, m_ref, alpha_ref, st_ref, pt_ref):
    p = pl.program_id(0)
    i = pl.program_id(1)
    tq, tk = FOX_TILE, FOX_KEY_TILE
    dh = FOX_HEAD_DIM
    H = FOX_HEADS

    NH = FOX_STEP_HEADS
    row = lax.broadcasted_iota(jnp.int32, (LANES, tq), 0)
    for h in range(NH):
        head = NH * p + h
        pair, half = divmod(h, 2)
        qt = qt_ref[pair * LANES:(pair + 1) * LANES, :].astype(F32)
        q_part = jnp.where((row >= half * dh) & (row < (half + 1) * dh), qt, 0.0)
        gate_rows = (row == head) | (row == H + head) | (row == 2 * H + head)
        c_parts = [ct_ref[pl.ds(part * H + head, 1), :] for part in range(3)]
        aug = jnp.where(gate_rows, 1.0, 0.0)
        for part in range(3):
            aug = jnp.where(row == FOX_ONES_ROW + part, c_parts[part], aug)
        rhs_ref[h] = jnp.concatenate([q_part, aug], axis=0).astype(BF16)

    acc_ref[...] = jnp.zeros_like(acc_ref)
    m_ref[...] = jnp.full_like(m_ref, NEG_BIG)
    pt_ref[1] = jnp.zeros(pt_ref.shape[1:], BF16)
    alpha_ref[1] = jnp.ones(alpha_ref.shape[1:], F32)

    def logits(j, slot):
        start = pl.multiple_of(j * tk, tk)
        gates = kaug_ref[pl.ds(start, tk), :]
        for h in range(NH):
            pair = h // 2
            lhs = jnp.concatenate([k_ref[pl.ds(start, tk), pair * LANES:(pair + 1) * LANES], gates], axis=1)
            st_ref[slot, h] = jnp.dot(lhs, rhs_ref[h], preferred_element_type=F32)

    def softmax(slot, mask_offset=None):
        for h in range(NH):
            st = st_ref[slot, h]
            if mask_offset is not None:
                causal = (lax.broadcasted_iota(jnp.int32, (tk, tq), 0) + mask_offset
                          <= lax.broadcasted_iota(jnp.int32, (tk, tq), 1))
                st = jnp.where(causal, st, NEG_BIG)
            m_old = m_ref[h]
            m_new = jnp.maximum(m_old, jnp.max(st, axis=0, keepdims=True))
            alpha_ref[slot, h] = jnp.exp2(m_old - m_new)
            pt_ref[slot, h] = jnp.exp2(st - m_new).astype(BF16)
            m_ref[h] = m_new

    ones_rows = jnp.ones((FOX_VROWS - dh, tk), BF16)

    def accum(j, slot):
        start = pl.multiple_of(j * tk, tk)
        for h in range(NH):
            vt_aug = jnp.concatenate([vt_ref[h * dh:(h + 1) * dh, pl.ds(start, tk)], ones_rows], axis=0)
            pv = jnp.dot(vt_aug, pt_ref[slot, h], preferred_element_type=F32)
            acc_ref[h] = alpha_ref[slot, h] * acc_ref[h] + pv

    sub = tq // tk

    def body(p, carry):
        accum(jnp.maximum(sub * p - 1, 0), 1)
        softmax(0)
        logits(sub * p + 1, 1)
        accum(sub * p, 0)
        softmax(1)
        logits(sub * p + 2, 0)
        return carry

    first = first_ref[p, i]
    logits(sub * first, 0)
    lax.fori_loop(first, i, body, 0)
    logits(sub * i + 1, 1)
    accum(jnp.maximum(sub * i - 1, 0), 1)
    softmax(0, mask_offset=0)
    accum(sub * i, 0)
    softmax(1, mask_offset=tk)
    accum(sub * i + 1, 1)

    for pair in range(NH // 2):
        acc_a, acc_b = acc_ref[2 * pair], acc_ref[2 * pair + 1]
        ot = jnp.concatenate([acc_a[:dh] / acc_a[dh:dh + 1], acc_b[:dh] / acc_b[dh:dh + 1]], axis=0)
        o_ref[:, pair * LANES:(pair + 1) * LANES] = ot.T.astype(o_ref.dtype)


def _fox_norms_kernel(k_ref, qt_ref, kn_ref, qn_ref):
    n = FOX_HEADS * FOX_HEAD_DIM
    chan = lax.broadcasted_iota(jnp.int32, (n, LANES), 0) // FOX_HEAD_DIM
    head = lax.broadcasted_iota(jnp.int32, (n, LANES), 1)
    group = (chan == head).astype(BF16)
    k = k_ref[...].astype(F32)
    k2 = jnp.dot((k * k).astype(BF16), group, preferred_element_type=F32)
    q = qt_ref[...].astype(F32)
    q2 = lax.dot_general((q * q).astype(BF16), group, (((0,), (0,)), ((), ())), preferred_element_type=F32)
    kn_ref[0] = jnp.broadcast_to(jnp.max(k2, axis=0, keepdims=True), kn_ref.shape[1:])
    qn_ref[0] = jnp.broadcast_to(jnp.max(q2, axis=0, keepdims=True), qn_ref.shape[1:])


def fox_norms(k, qt):
    T = k.shape[0]
    t = FOX_TILE
    n = FOX_HEADS * FOX_HEAD_DIM
    out = pl.BlockSpec((1, 8, LANES), lambda i: (i, 0, 0))
    return pl.pallas_call(
        _fox_norms_kernel,
        grid=(T // t,),
        in_specs=[pl.BlockSpec((t, n), lambda i: (i, 0)), pl.BlockSpec((n, t), lambda i: (0, i))],
        out_specs=[out, out],
        out_shape=[jax.ShapeDtypeStruct((T // t, 8, LANES), F32)] * 2,
        compiler_params=_cparams(("parallel",)),
        name="fox_norms",
    )(k, qt)


FOX_UNDERFLOW_LOG2 = 160.0


def fox_first_block(kn2, qn2, ct):
    H = FOX_HEADS
    nt = kn2.shape[0]
    bk = jnp.sqrt(jnp.max(kn2[:, 0, :H], axis=0))
    bq = jnp.sqrt(qn2[:, 0, :H])
    need = 2.0 * 1.02 * bq * bk[None, :] + FOX_UNDERFLOW_LOG2 + 2.0
    c_first = ct[:, 0::FOX_TILE]
    c_last = ct[:, FOX_TILE - 1::FOX_TILE]
    c_first = c_first[:H] + c_first[H:2 * H] + c_first[2 * H:3 * H]
    c_last = c_last[:H] + c_last[H:2 * H] + c_last[2 * H:3 * H]
    gap = c_last[:, None, :] - c_first[:, :, None]
    earlier = jnp.arange(nt)[None, :] < jnp.arange(nt)[:, None]
    prunable = (gap > need.T[:, :, None]) & earlier[None]
    n_skip = jnp.min(jnp.where(prunable, nt, jnp.arange(nt)[None, None, :]), axis=2)
    return jnp.min(n_skip.reshape(H // FOX_STEP_HEADS, FOX_STEP_HEADS, nt), axis=1).astype(jnp.int32)


def fox_attention(first, k, kaug, qt, ct, vt):
    T = k.shape[0]
    t = FOX_TILE
    nh = FOX_STEP_HEADS
    width = nh * FOX_HEAD_DIM
    return pl.pallas_call(
        _fox_kernel,
        grid=(FOX_HEADS // nh, T // t),
        in_specs=[
            pl.BlockSpec(memory_space=pltpu.SMEM),
            pl.BlockSpec((T, width), lambda p, i: (0, p)),
            pl.BlockSpec((T, LANES), lambda p, i: (0, 0), pipeline_mode=pl.Buffered(1)),
            pl.BlockSpec((width, t), lambda p, i: (p, i)),
            pl.BlockSpec((LANES, t), lambda p, i: (0, i)),
            pl.BlockSpec((width, T), lambda p, i: (p, 0)),
        ],
        out_specs=pl.BlockSpec((t, width), lambda p, i: (i, p)),
        out_shape=jax.ShapeDtypeStruct((T, FOX_HEADS * FOX_HEAD_DIM), BF16),
        scratch_shapes=[
            pltpu.VMEM((nh, 2 * LANES, t), BF16),
            pltpu.VMEM((nh, FOX_VROWS, t), F32),
            pltpu.VMEM((nh, 1, t), F32),
            pltpu.VMEM((2, nh, 1, t), F32),
            pltpu.VMEM((2, nh, FOX_KEY_TILE, t), F32),
            pltpu.VMEM((2, nh, FOX_KEY_TILE, t), BF16),
        ],
        compiler_params=_cparams(("parallel", "arbitrary")),
        name="fox",
    )(first, k, kaug, qt, ct, vt)


def kernel(x, norm_mix, norm_mlp, norm_final, w_up, w_down, swa_w_qkv, swa_b_qkv, swa_sinks, swa_w_o,
           hgrn_w_in, hgrn_lb_logits, hgrn_g_norm, hgrn_w_o, fox_w_in, fox_b_in, fox_w_o):
    B, T, D = x.shape
    depth = norm_mix.shape[0]
    lb_soft = jax.nn.softmax(hgrn_lb_logits.astype(F32), axis=0)
    lower_bounds = jnp.cumsum(lb_soft, axis=0) - lb_soft[0]
    w_up_bf16 = w_up.astype(BF16)
    w_down_bf16 = w_down.astype(BF16)

    outs = []
    for b in range(B):
        xb = x[b]
        for i in range(depth):
            m, j = i % N_MIXERS, i // N_MIXERS
            if m == 0:
                nq = SWA_HEADS * SWA_HEAD_DIM
                nkv = SWA_KV_HEADS * SWA_HEAD_DIM
                col_scale = jnp.concatenate([jnp.full((nq,), LOG2E * SWA_HEAD_DIM ** -0.5, F32),
                                             jnp.ones((2 * nkv,), F32)])
                qt, k, vt = norm_proj(xb, norm_mix[i], (swa_w_qkv[j] * col_scale).astype(BF16),
                                      swa_b_qkv[j] * col_scale,
                                      [(0, nq), (nq, nq + nkv), (nq + nkv, nq + 2 * nkv)], [BF16] * 3, "swa_proj",
                                      transposed=(True, False, True))
                a = swa_attention(qt, k, vt, swa_sinks[j])
                w_o = swa_w_o[j]
            elif m == 1:
                hk = HGRN_HEADS * HGRN_EXPAND
                hv = HGRN_HEADS * HGRN_HEAD_DIM
                q, f, v, g = norm_proj(xb, norm_mix[i], hgrn_w_in[j].astype(BF16), jnp.zeros((2 * hk + 2 * hv,), F32),
                                       [(0, hk), (hk, 2 * hk), (2 * hk, 2 * hk + hv), (2 * hk + hv, 2 * hk + 2 * hv)],
                                       [F32, F32, BF16, F32], "hgrn_proj")
                o = hgrn_recurrence(q, f, v, lower_bounds[i])
                a = (o, g, hgrn_g_norm[j])
                w_o = hgrn_w_o[j]
            else:
                n = FOX_HEADS * FOX_HEAD_DIM
                pad = LANES - FOX_HEADS
                col_scale = jnp.concatenate([jnp.full((n,), LOG2E * FOX_HEAD_DIM ** -0.5, F32),
                                             jnp.ones((2 * n + FOX_HEADS,), F32)])
                w = jnp.pad(fox_w_in[j] * col_scale, ((0, 0), (0, pad))).astype(BF16)
                bias = jnp.pad(fox_b_in[j] * col_scale, ((0, pad),))
                qt, k, vt, f = norm_proj(xb, norm_mix[i], w, bias,
                                         [(0, n), (n, 2 * n), (2 * n, 3 * n), (3 * n, 3 * n + LANES)],
                                         [BF16, BF16, BF16, F32], "fox_proj",
                                         transposed=(True, False, True, False))
                kaug, ct = fox_cumsum(f)
                first = fox_first_block(*fox_norms(k, qt), ct)
                a = fox_attention(first, k, kaug, qt, ct, vt)
                w_o = fox_w_o[j]
            last = i == depth - 1
            xb = out_mlp(a, w_o.astype(BF16), xb, norm_mlp[i], w_up_bf16, w_down_bf16, i, norm_final, last)
        outs.append(xb)
    return jnp.stack(outs, axis=0)
```

```python
import functools

import jax
import jax.numpy as jnp
from jax import lax
from jax.experimental import pallas as pl
from jax.experimental.pallas import tpu as pltpu

F32 = jnp.float32
BF16 = jnp.bfloat16

D_MODEL = 1024
D_FF = 4 * D_MODEL
EPS = 1e-6
N_MIXERS = 3

SWA_HEADS = 16
SWA_KV_HEADS = 4
SWA_HEAD_DIM = 64
SWA_WINDOW = 128

HGRN_HEADS = 8
HGRN_EXPAND = 128
HGRN_HEAD_DIM = 128
HGRN_CHUNK = 64
HGRN_SUB = 8
HGRN_STEP_CHUNKS = 2

FOX_HEADS = 16
FOX_HEAD_DIM = 64

LANES = 128
VMEM_LIMIT = 56 * 1024 * 1024
NEG_BIG = -1e30
LOG2E = 1.4426950408889634

ROW_TILE = 1024


def _cparams(sem):
    return pltpu.CompilerParams(dimension_semantics=sem, vmem_limit_bytes=VMEM_LIMIT)


def _const_spec(shape):
    nd = len(shape)
    return pl.BlockSpec(shape, lambda *_: (0,) * nd, pipeline_mode=pl.Buffered(1))


def _layer_spec(shape, layer):
    nd = len(shape)
    return pl.BlockSpec((None,) + tuple(shape), lambda *_: (layer,) + (0,) * nd, pipeline_mode=pl.Buffered(1))


def _rms(x, g):
    ms = jnp.mean(x * x, axis=-1, keepdims=True)
    return x * lax.rsqrt(ms + EPS) * g


def _norm_proj_kernel(x_ref, g_ref, w_ref, b_ref, *o_refs, segments, transposed, n_chunk):
    h = _rms(x_ref[...], g_ref[...]).astype(BF16)
    for o_ref, (lo, hi), tr in zip(o_refs, segments, transposed):
        for j in range(lo, hi, n_chunk):
            je = min(j + n_chunk, hi)
            y = jnp.dot(h, w_ref[:, j:je], preferred_element_type=F32) + b_ref[:, j:je]
            if tr:
                o_ref[j - lo:je - lo, :] = y.T.astype(o_ref.dtype)
            else:
                o_ref[:, j - lo:je - lo] = y.astype(o_ref.dtype)


def norm_proj(x, g, w, b, segments, dtypes, name, transposed=None):
    T, D = x.shape
    N = w.shape[1]
    tm = ROW_TILE
    transposed = tuple(transposed) if transposed is not None else (False,) * len(segments)
    out_shape, out_specs = [], []
    for (lo, hi), dt, tr in zip(segments, dtypes, transposed):
        if tr:
            out_shape.append(jax.ShapeDtypeStruct((hi - lo, T), dt))
            out_specs.append(pl.BlockSpec((hi - lo, tm), lambda i: (0, i)))
        else:
            out_shape.append(jax.ShapeDtypeStruct((T, hi - lo), dt))
            out_specs.append(pl.BlockSpec((tm, hi - lo), lambda i: (i, 0)))
    return pl.pallas_call(
        functools.partial(_norm_proj_kernel, segments=tuple(segments), transposed=transposed, n_chunk=512),
        grid=(T // tm,),
        in_specs=[
            pl.BlockSpec((tm, D), lambda i: (i, 0)),
            _const_spec((1, D)),
            _const_spec((D, N)),
            _const_spec((1, N)),
        ],
        out_specs=out_specs,
        out_shape=out_shape,
        compiler_params=_cparams(("parallel",)),
        name=name,
    )(x, g.reshape(1, D), w, b.reshape(1, N))


def _hgrn_gate(r_ref, gate_ref, gain_ref):
    parts = []
    for h in range(HGRN_HEADS):
        sl = slice(h * HGRN_HEAD_DIM, (h + 1) * HGRN_HEAD_DIM)
        r = r_ref[:, sl]
        r = r * lax.rsqrt(jnp.mean(r * r, axis=-1, keepdims=True) + EPS)
        g = gate_ref[:, sl]
        parts.append((r * gain_ref[:, sl] * (g * jax.nn.sigmoid(g))).astype(BF16))
    return jnp.concatenate(parts, axis=1)


def _mlp_tail(x, g_ref, wu_ref, wd_ref, gf_ref, o_ref, ff_chunk, final_norm):
    h = _rms(x, g_ref[...]).astype(BF16)
    acc = x
    for j in range(0, D_FF, ff_chunk):
        u = jnp.dot(h, wu_ref[:, j:j + ff_chunk], preferred_element_type=F32)
        u = jnp.maximum(u, 0.0)
        u = (u * u).astype(BF16)
        acc = acc + jnp.dot(u, wd_ref[j:j + ff_chunk, :], preferred_element_type=F32)
    if final_norm:
        acc = _rms(acc, gf_ref[...])
    o_ref[...] = acc


def _out_mlp_kernel(a_ref, wo_ref, x_ref, g_ref, wu_ref, wd_ref, gf_ref, o_ref, *, ff_chunk, final_norm):
    x = x_ref[...] + jnp.dot(a_ref[...], wo_ref[...], preferred_element_type=F32)
    _mlp_tail(x, g_ref, wu_ref, wd_ref, gf_ref, o_ref, ff_chunk, final_norm)


def _gated_out_mlp_kernel(r_ref, gate_ref, gain_ref, wo_ref, x_ref, g_ref, wu_ref, wd_ref, gf_ref, o_ref, *,
                          ff_chunk, final_norm):
    a = _hgrn_gate(r_ref, gate_ref, gain_ref)
    x = x_ref[...] + jnp.dot(a, wo_ref[...], preferred_element_type=F32)
    _mlp_tail(x, g_ref, wu_ref, wd_ref, gf_ref, o_ref, ff_chunk, final_norm)


def out_mlp(mixer_out, w_o, x, g, w_up_all, w_down_all, layer, g_final, final_norm):
    T, D = x.shape
    gated = isinstance(mixer_out, tuple)
    tm = ROW_TILE // 2 if gated else ROW_TILE
    row = pl.BlockSpec((tm, D), lambda i: (i, 0))
    if gated:
        r, gate, gain = mixer_out
        body, mix_args, mix_specs = _gated_out_mlp_kernel, (r, gate, gain.reshape(1, D)), [row, row, _const_spec((1, D))]
    else:
        body, mix_args, mix_specs = _out_mlp_kernel, (mixer_out,), [row]
    return pl.pallas_call(
        functools.partial(body, ff_chunk=512, final_norm=final_norm),
        grid=(T // tm,),
        in_specs=mix_specs + [
            _const_spec((D, D)),
            row,
            _const_spec((1, D)),
            _layer_spec((D, D_FF), layer),
            _layer_spec((D_FF, D), layer),
            _const_spec((1, D)),
        ],
        out_specs=row,
        out_shape=jax.ShapeDtypeStruct((T, D), F32),
        compiler_params=_cparams(("parallel",)),
        name="out_mlp",
    )(*mix_args, w_o, x, g.reshape(1, D), w_up_all, w_down_all, g_final.reshape(1, D))


SWA_VROWS = SWA_HEAD_DIM + 16
SWA_STEP_BLOCKS = 4


def _swa_kernel(sink_ref, qt_ref, kc_ref, kp_ref, vtc_ref, vtp_ref, o_ref):
    n = pl.program_id(0)
    W, dh, NB = SWA_WINDOW, SWA_HEAD_DIM, SWA_STEP_BLOCKS
    G = SWA_HEADS // SWA_KV_HEADS
    key = lax.broadcasted_iota(jnp.int32, (2 * W, G * W), 0)
    lane = lax.broadcasted_iota(jnp.int32, (2 * W, G * W), 1)
    rel = lane % W + W - key
    in_window = (rel >= 0) & (rel < W)
    group_lane = lax.broadcasted_iota(jnp.int32, (1, G * W), 1) // W
    k_all = jnp.concatenate([kp_ref[...], kc_ref[...]], axis=0)
    vt_all = jnp.concatenate([vtp_ref[...], vtc_ref[...]], axis=1)
    ones_rows = jnp.ones((SWA_VROWS - dh, 2 * W), BF16)
    units = [(b, hk) for b in range(NB) for hk in range(SWA_KV_HEADS)]

    sts = []
    for b, hk in units:
        pair, half = divmod(hk, 2)
        lhs = k_all[b * W:(b + 2) * W, pair * LANES:(pair + 1) * LANES]
        q_cat = jnp.concatenate([qt_ref[(hk * G + g) * dh:(hk * G + g + 1) * dh, b * W:(b + 1) * W]
                                 for g in range(G)], axis=1)
        zeros = jnp.zeros_like(q_cat)
        rhs = jnp.concatenate([q_cat, zeros] if half == 0 else [zeros, q_cat], axis=0)
        sts.append(jnp.dot(lhs, rhs, preferred_element_type=F32))

    pts, sink_terms = [], []
    for (b, hk), st in zip(units, sts):
        visible = in_window if b > 0 else in_window & ((n > 0) | (key >= W))
        st = jnp.where(visible, st, NEG_BIG)
        sink = jnp.zeros((1, G * W), F32)
        for g in range(G):
            sink = jnp.where(group_lane == g, sink_ref[hk * G + g] * LOG2E, sink)
        m = jnp.maximum(jnp.max(st, axis=0, keepdims=True), sink)
        pts.append(jnp.exp2(st - m).astype(BF16))
        sink_terms.append(jnp.exp2(sink - m))

    accs = []
    for (b, hk), pt in zip(units, pts):
        vt_aug = jnp.concatenate([vt_all[hk * dh:(hk + 1) * dh, b * W:(b + 2) * W], ones_rows], axis=0)
        accs.append(jnp.dot(vt_aug, pt, preferred_element_type=F32))

    outs = [[] for _ in range(NB)]
    for (b, hk), acc, sink_term in zip(units, accs, sink_terms):
        ot = acc[:dh] / (acc[dh:dh + 1] + sink_term)
        for g in range(0, G, 2):
            two = jnp.concatenate([ot[:, g * W:(g + 1) * W], ot[:, (g + 1) * W:(g + 2) * W]], axis=0)
            outs[b].append(two.T)
    for b in range(NB):
        o_ref[b * W:(b + 1) * W, :] = jnp.concatenate(outs[b], axis=1).astype(o_ref.dtype)


def swa_attention(qt, k, vt, sinks):
    T = k.shape[0]
    W, NB = SWA_WINDOW, SWA_STEP_BLOCKS
    nq = SWA_HEADS * SWA_HEAD_DIM
    kvw = SWA_KV_HEADS * SWA_HEAD_DIM
    prev = lambda n: jnp.maximum(NB * n - 1, 0)
    return pl.pallas_call(
        _swa_kernel,
        grid=(T // (NB * W),),
        in_specs=[
            pl.BlockSpec(memory_space=pltpu.SMEM),
            pl.BlockSpec((nq, NB * W), lambda n: (0, n)),
            pl.BlockSpec((NB * W, kvw), lambda n: (n, 0)),
            pl.BlockSpec((W, kvw), lambda n: (prev(n), 0)),
            pl.BlockSpec((kvw, NB * W), lambda n: (0, n)),
            pl.BlockSpec((kvw, W), lambda n: (0, prev(n))),
        ],
        out_specs=pl.BlockSpec((NB * W, nq), lambda n: (n, 0)),
        out_shape=jax.ShapeDtypeStruct((T, nq), BF16),
        compiler_params=_cparams(("parallel",)),
        name="swa",
    )(sinks, qt, k, k, vt, vt)


def _split2(x):
    hi = x.astype(BF16)
    lo = (x - hi.astype(F32)).astype(BF16)
    return hi, lo


def _split_dot(m_bf16, parts):
    hi, lo = parts
    return (jnp.dot(m_bf16, hi, preferred_element_type=F32)
            + jnp.dot(m_bf16, lo, preferred_element_type=F32))


def _hgrn_kernel(q_ref, f_ref, v_ref, lb_ref, o_ref, st_ref):
    C, SUB, K, H = HGRN_CHUNK, HGRN_SUB, HGRN_EXPAND, HGRN_HEADS
    R = C * HGRN_STEP_CHUNKS
    nt = (((1,), (1,)), ((), ()))
    tn = (((0,), (0,)), ((), ()))

    @pl.when(pl.program_id(0) == 0)
    def _():
        st_ref[...] = jnp.zeros_like(st_ref)

    row = lax.broadcasted_iota(jnp.int32, (R, R), 0)
    col = lax.broadcasted_iota(jnp.int32, (R, R), 1)
    same_chunk = row // C == col // C
    heads = [slice(h * K, (h + 1) * K) for h in range(H)]
    chunks = [slice(c * C, (c + 1) * C) for c in range(HGRN_STEP_CHUNKS)]

    lb = lb_ref[...]
    f = lb + (1.0 - lb) * jax.nn.sigmoid(f_ref[...])
    logf = _split2(jnp.log2(f))
    kk = 1.0 - f
    q = q_ref[...]
    qs = q * jax.nn.sigmoid(q)
    v = v_ref[...]
    bcum = _split_dot(((row >= col) & same_chunk).astype(BF16), logf)
    to_end = _split_dot(((row < col) & same_chunk).astype(BF16), logf)

    q_in = (qs * jnp.exp2(bcum)).astype(BF16)
    k_out = (kk * jnp.exp2(to_end)).astype(BF16)
    o_in = [[] for _ in heads]
    for cs in chunks:
        decay = jnp.exp2(bcum[cs.stop - 1:cs.stop, :])
        for h, hs in enumerate(heads):
            st = st_ref[h]
            o_in[h].append(lax.dot_general(q_in[cs, hs], st.astype(BF16), nt, preferred_element_type=F32))
            upd = lax.dot_general(v[cs, hs], k_out[cs, hs], tn, preferred_element_type=F32)
            st_ref[h] = st * decay[:, hs] + upd
    o = [jnp.concatenate(parts, axis=0) for parts in o_in]


    a = [jnp.zeros((R, R), F32) for _ in heads]
    half = SUB
    while half < C:
        pos = row % (2 * half)
        bnd = row - pos + half - 1
        to_bnd = (((pos >= half) & (col > bnd) & (col <= row))
                  | ((pos < half) & (col > row) & (col <= bnd)))
        e = jnp.exp2(_split_dot(to_bnd.astype(BF16), logf))
        q_e = (qs * e).astype(BF16)
        k_e = (kk * e).astype(BF16)
        right_left = (pos >= half) & (col <= bnd) & (col > bnd - half)
        for h, hs in enumerate(heads):
            a_l = lax.dot_general(q_e[:, hs], k_e[:, hs], nt, preferred_element_type=F32)
            a[h] = jnp.where(right_left, a_l, a[h])
        half *= 2

    sub_pos = lax.broadcasted_iota(jnp.int32, (R, H * K), 0) % SUB
    for d in range(SUB):
        if d == 0:
            e = qs * kk
        else:
            e = qs * pltpu.roll(kk, d, 0) * jnp.exp2(bcum - pltpu.roll(bcum, d, 0))
            e = jnp.where(sub_pos >= d, e, 0.0)
        band = (row - col == d) & (row % SUB >= d)
        for h, hs in enumerate(heads):
            a[h] = jnp.where(band, jnp.sum(e[:, hs], axis=1, keepdims=True), a[h])

    for h, hs in enumerate(heads):
        o[h] = o[h] + jnp.dot(a[h].astype(BF16), v[:, hs], preferred_element_type=F32)
    o_ref[...] = jnp.concatenate(o, axis=1)


def hgrn_recurrence(q, f_logit, v, lb):
    T = q.shape[0]
    rows = HGRN_CHUNK * HGRN_STEP_CHUNKS
    width = HGRN_HEADS * HGRN_EXPAND
    blk = pl.BlockSpec((rows, width), lambda c: (c, 0))
    return pl.pallas_call(
        _hgrn_kernel,
        grid=(T // rows,),
        in_specs=[blk, blk, blk, _const_spec((1, width))],
        out_specs=blk,
        out_shape=jax.ShapeDtypeStruct((T, width), F32),
        scratch_shapes=[pltpu.VMEM((HGRN_HEADS, HGRN_HEAD_DIM, HGRN_EXPAND), F32)],
        compiler_params=_cparams(("arbitrary",)),
        name="hgrn",
    )(q, f_logit, v, lb.reshape(1, -1))


FOX_CUM_TILE = 512


def _split3(x):
    hi = x.astype(BF16)
    r1 = x - hi.astype(F32)
    mid = r1.astype(BF16)
    lo = (r1 - mid.astype(F32)).astype(BF16)
    return hi, mid, lo


FOX_ONES_ROW = 3 * FOX_HEADS


def _fox_cumsum_kernel(f_ref, kaug_ref, ct_ref, carry_ref):
    @pl.when(pl.program_id(0) == 0)
    def _():
        carry_ref[...] = jnp.zeros_like(carry_ref)

    x = f_ref[...]
    ls = jnp.minimum(x, 0.0) - jnp.log(1.0 + jnp.exp(-jnp.abs(x)))
    n = FOX_CUM_TILE
    tril = (lax.broadcasted_iota(jnp.int32, (n, n), 0) >= lax.broadcasted_iota(jnp.int32, (n, n), 1)).astype(BF16)
    hi, mid, lo = _split3(ls)
    cum = (jnp.dot(tril, hi, preferred_element_type=F32)
           + jnp.dot(tril, mid, preferred_element_type=F32)
           + jnp.dot(tril, lo, preferred_element_type=F32))
    c = cum + carry_ref[...]
    carry_ref[...] = c[n - 1:n, :]
    H = FOX_HEADS
    hi2, mid2, lo2 = (a.astype(F32) for a in _split3(c * LOG2E))
    lane = lax.broadcasted_iota(jnp.int32, (n, LANES), 1)
    packed = jnp.where(lane < H, hi2,
                       jnp.where(lane < 2 * H, pltpu.roll(mid2, H, 1),
                                 jnp.where(lane < 3 * H, pltpu.roll(lo2, 2 * H, 1), 0.0)))
    is_one = (lane >= FOX_ONES_ROW) & (lane < FOX_ONES_ROW + 3)
    kaug_ref[...] = jnp.where(is_one, 1.0, -packed).astype(BF16)
    ct_ref[...] = packed.T


def fox_cumsum(f_logit):
    T = f_logit.shape[0]
    n = FOX_CUM_TILE
    spec = pl.BlockSpec((n, LANES), lambda i: (i, 0))
    return pl.pallas_call(
        _fox_cumsum_kernel,
        grid=(T // n,),
        in_specs=[spec],
        out_specs=[spec, pl.BlockSpec((LANES, n), lambda i: (0, i))],
        out_shape=[jax.ShapeDtypeStruct((T, LANES), BF16), jax.ShapeDtypeStruct((LANES, T), F32)],
        scratch_shapes=[pltpu.VMEM((1, LANES), F32)],
        compiler_params=_cparams(("arbitrary",)),
        name="fox_cumsum",
    )(f_logit)


FOX_TILE = 512
FOX_KEY_TILE = 256
FOX_STEP_HEADS = 2


FOX_VROWS = FOX_HEAD_DIM + 16


def _fox_kernel(first_ref, k_ref, kaug_ref, qt_ref, ct_ref, vt_ref, o_ref,
                rhs_ref, acc_ref, m_ref, alpha_ref, st_ref, pt_ref):
    p = pl.program_id(0)
    i = pl.program_id(1)
    tq, tk = FOX_TILE, FOX_KEY_TILE
    dh = FOX_HEAD_DIM
    H = FOX_HEADS

    NH = FOX_STEP_HEADS
    row = lax.broadcasted_iota(jnp.int32, (LANES, tq), 0)
    for h in range(NH):
        head = NH * p + h
        pair, half = divmod(h, 2)
        qt = qt_ref[pair * LANES:(pair + 1) * LANES, :].astype(F32)
        q_part = jnp.where((row >= half * dh) & (row < (half + 1) * dh), qt, 0.0)
        gate_rows = (row == head) | (row == H + head) | (row == 2 * H + head)
        c_parts = [ct_ref[pl.ds(part * H + head, 1), :] for part in range(3)]
        aug = jnp.where(gate_rows, 1.0, 0.0)
        for part in range(3):
            aug = jnp.where(row == FOX_ONES_ROW + part, c_parts[part], aug)
        rhs_ref[h] = jnp.concatenate([q_part, aug], axis=0).astype(BF16)

    acc_ref[...] = jnp.zeros_like(acc_ref)
    m_ref[...] = jnp.full_like(m_ref, NEG_BIG)
    pt_ref[1] = jnp.zeros(pt_ref.shape[1:], BF16)
    alpha_ref[1] = jnp.ones(alpha_ref.shape[1:], F32)

    def logits(j, slot):
        start = pl.multiple_of(j * tk, tk)
        gates = kaug_ref[pl.ds(start, tk), :]
        for h in range(NH):
            pair = h // 2
            lhs = jnp.concatenate([k_ref[pl.ds(start, tk), pair * LANES:(pair + 1) * LANES], gates], axis=1)
            st_ref[slot, h] = jnp.dot(lhs, rhs_ref[h], preferred_element_type=F32)

    def softmax(slot, mask_offset=None):
        for h in range(NH):
            st = st_ref[slot, h]
            if mask_offset is not None:
                causal = (lax.broadcasted_iota(jnp.int32, (tk, tq), 0) + mask_offset
                          <= lax.broadcasted_iota(jnp.int32, (tk, tq), 1))
                st = jnp.where(causal, st, NEG_BIG)
            m_old = m_ref[h]
            m_new = jnp.maximum(m_old, jnp.max(st, axis=0, keepdims=True))
            alpha_ref[slot, h] = jnp.exp2(m_old - m_new)
            pt_ref[slot, h] = jnp.exp2(st - m_new).astype(BF16)
            m_ref[h] = m_new

    ones_rows = jnp.ones((FOX_VROWS - dh, tk), BF16)

    def accum(j, slot):
        start = pl.multiple_of(j * tk, tk)
        for h in range(NH):
            vt_aug = jnp.concatenate([vt_ref[h * dh:(h + 1) * dh, pl.ds(start, tk)], ones_rows], axis=0)
            pv = jnp.dot(vt_aug, pt_ref[slot, h], preferred_element_type=F32)
            acc_ref[h] = alpha_ref[slot, h] * acc_ref[h] + pv

    sub = tq // tk

    def body(p, carry):
        accum(jnp.maximum(sub * p - 1, 0), 1)
        softmax(0)
        logits(sub * p + 1, 1)
        accum(sub * p, 0)
        softmax(1)
        logits(sub * p + 2, 0)
        return carry

    first = first_ref[p, i]
    logits(sub * first, 0)
    lax.fori_loop(first, i, body, 0)
    logits(sub * i + 1, 1)
    accum(jnp.maximum(sub * i - 1, 0), 1)
    softmax(0, mask_offset=0)
    accum(sub * i, 0)
    softmax(1, mask_offset=tk)
    accum(sub * i + 1, 1)

    for pair in range(NH // 2):
        acc_a, acc_b = acc_ref[2 * pair], acc_ref[2 * pair + 1]
        ot = jnp.concatenate([acc_a[:dh] / acc_a[dh:dh + 1], acc_b[:dh] / acc_b[dh:dh + 1]], axis=0)
        o_ref[:, pair * LANES:(pair + 1) * LANES] = ot.T.astype(o_ref.dtype)


def _fox_norms_kernel(k_ref, qt_ref, kn_ref, qn_ref):
    n = FOX_HEADS * FOX_HEAD_DIM
    chan = lax.broadcasted_iota(jnp.int32, (n, LANES), 0) // FOX_HEAD_DIM
    head = lax.broadcasted_iota(jnp.int32, (n, LANES), 1)
    group = (chan == head).astype(BF16)
    k = k_ref[...].astype(F32)
    k2 = jnp.dot((k * k).astype(BF16), group, preferred_element_type=F32)
    q = qt_ref[...].astype(F32)
    q2 = lax.dot_general((q * q).astype(BF16), group, (((0,), (0,)), ((), ())), preferred_element_type=F32)
    kn_ref[0] = jnp.broadcast_to(jnp.max(k2, axis=0, keepdims=True), kn_ref.shape[1:])
    qn_ref[0] = jnp.broadcast_to(jnp.max(q2, axis=0, keepdims=True), qn_ref.shape[1:])


def fox_norms(k, qt):
    T = k.shape[0]
    t = FOX_TILE
    n = FOX_HEADS * FOX_HEAD_DIM
    out = pl.BlockSpec((1, 8, LANES), lambda i: (i, 0, 0))
    return pl.pallas_call(
        _fox_norms_kernel,
        grid=(T // t,),
        in_specs=[pl.BlockSpec((t, n), lambda i: (i, 0)), pl.BlockSpec((n, t), lambda i: (0, i))],
        out_specs=[out, out],
        out_shape=[jax.ShapeDtypeStruct((T // t, 8, LANES), F32)] * 2,
        compiler_params=_cparams(("parallel",)),
        name="fox_norms",
    )(k, qt)


FOX_UNDERFLOW_LOG2 = 160.0


def fox_first_block(kn2, qn2, ct):
    H = FOX_HEADS
    nt = kn2.shape[0]
    bk = jnp.sqrt(jnp.max(kn2[:, 0, :H], axis=0))
    bq = jnp.sqrt(qn2[:, 0, :H])
    need = 2.0 * 1.02 * bq * bk[None, :] + FOX_UNDERFLOW_LOG2 + 2.0
    c_first = ct[:, 0::FOX_TILE]
    c_last = ct[:, FOX_TILE - 1::FOX_TILE]
    c_first = c_first[:H] + c_first[H:2 * H] + c_first[2 * H:3 * H]
    c_last = c_last[:H] + c_last[H:2 * H] + c_last[2 * H:3 * H]
    gap = c_last[:, None, :] - c_first[:, :, None]
    earlier = jnp.arange(nt)[None, :] < jnp.arange(nt)[:, None]
    prunable = (gap > need.T[:, :, None]) & earlier[None]
    n_skip = jnp.min(jnp.where(prunable, nt, jnp.arange(nt)[None, None, :]), axis=2)
    return jnp.min(n_skip.reshape(H // FOX_STEP_HEADS, FOX_STEP_HEADS, nt), axis=1).astype(jnp.int32)


def fox_attention(first, k, kaug, qt, ct, vt):
    T = k.shape[0]
    t = FOX_TILE
    nh = FOX_STEP_HEADS
    width = nh * FOX_HEAD_DIM
    return pl.pallas_call(
        _fox_kernel,
        grid=(FOX_HEADS // nh, T // t),
        in_specs=[
            pl.BlockSpec(memory_space=pltpu.SMEM),
            pl.BlockSpec((T, width), lambda p, i: (0, p)),
            pl.BlockSpec((T, LANES), lambda p, i: (0, 0), pipeline_mode=pl.Buffered(1)),
            pl.BlockSpec((width, t), lambda p, i: (p, i)),
            pl.BlockSpec((LANES, t), lambda p, i: (0, i)),
            pl.BlockSpec((width, T), lambda p, i: (p, 0)),
        ],
        out_specs=pl.BlockSpec((t, width), lambda p, i: (i, p)),
        out_shape=jax.ShapeDtypeStruct((T, FOX_HEADS * FOX_HEAD_DIM), BF16),
        scratch_shapes=[
            pltpu.VMEM((nh, 2 * LANES, t), BF16),
            pltpu.VMEM((nh, FOX_VROWS, t), F32),
            pltpu.VMEM((nh, 1, t), F32),
            pltpu.VMEM((2, nh, 1, t), F32),
            pltpu.VMEM((2, nh, FOX_KEY_TILE, t), F32),
            pltpu.VMEM((2, nh, FOX_KEY_TILE, t), BF16),
        ],
        compiler_params=_cparams(("parallel", "arbitrary")),
        name="fox",
    )(first, k, kaug, qt, ct, vt)


def kernel(x, norm_mix, norm_mlp, norm_final, w_up, w_down, swa_w_qkv, swa_b_qkv, swa_sinks, swa_w_o,
           hgrn_w_in, hgrn_lb_logits, hgrn_g_norm, hgrn_w_o, fox_w_in, fox_b_in, fox_w_o):
    B, T, D = x.shape
    depth = norm_mix.shape[0]
    lb_soft = jax.nn.softmax(hgrn_lb_logits.astype(F32), axis=0)
    lower_bounds = jnp.cumsum(lb_soft, axis=0) - lb_soft[0]
    w_up_bf16 = w_up.astype(BF16)
    w_down_bf16 = w_down.astype(BF16)

    outs = []
    for b in range(B):
        xb = x[b]
        for i in range(depth):
            m, j = i % N_MIXERS, i // N_MIXERS
            if m == 0:
                nq = SWA_HEADS * SWA_HEAD_DIM
                nkv = SWA_KV_HEADS * SWA_HEAD_DIM
                col_scale = jnp.concatenate([jnp.full((nq,), LOG2E * SWA_HEAD_DIM ** -0.5, F32),
                                             jnp.ones((2 * nkv,), F32)])
                qt, k, vt = norm_proj(xb, norm_mix[i], (swa_w_qkv[j] * col_scale).astype(BF16),
                                      swa_b_qkv[j] * col_scale,
                                      [(0, nq), (nq, nq + nkv), (nq + nkv, nq + 2 * nkv)], [BF16] * 3, "swa_proj",
                                      transposed=(True, False, True))
                a = swa_attention(qt, k, vt, swa_sinks[j])
                w_o = swa_w_o[j]
            elif m == 1:
                hk = HGRN_HEADS * HGRN_EXPAND
                hv = HGRN_HEADS * HGRN_HEAD_DIM
                q, f, v, g = norm_proj(xb, norm_mix[i], hgrn_w_in[j].astype(BF16), jnp.zeros((2 * hk + 2 * hv,), F32),
                                       [(0, hk), (hk, 2 * hk), (2 * hk, 2 * hk + hv), (2 * hk + hv, 2 * hk + 2 * hv)],
                                       [F32, F32, BF16, F32], "hgrn_proj")
                o = hgrn_recurrence(q, f, v, lower_bounds[i])
                a = (o, g, hgrn_g_norm[j])
                w_o = hgrn_w_o[j]
            else:
                n = FOX_HEADS * FOX_HEAD_DIM
                pad = LANES - FOX_HEADS
                col_scale = jnp.concatenate([jnp.full((n,), LOG2E * FOX_HEAD_DIM ** -0.5, F32),
                                             jnp.ones((2 * n + FOX_HEADS,), F32)])
                w = jnp.pad(fox_w_in[j] * col_scale, ((0, 0), (0, pad))).astype(BF16)
                bias = jnp.pad(fox_b_in[j] * col_scale, ((0, pad),))
                qt, k, vt, f = norm_proj(xb, norm_mix[i], w, bias,
                                         [(0, n), (n, 2 * n), (2 * n, 3 * n), (3 * n, 3 * n + LANES)],
                                         [BF16, BF16, BF16, F32], "fox_proj",
                                         transposed=(True, False, True, False))
                kaug, ct = fox_cumsum(f)
                first = fox_first_block(*fox_norms(k, qt), ct)
                a = fox_attention(first, k, kaug, qt, ct, vt)
                w_o = fox_w_o[j]
            last = i == depth - 1
            xb = out_mlp(a, w_o.astype(BF16), xb, norm_mlp[i], w_up_bf16, w_down_bf16, i, norm_final, last)
        outs.append(xb)
    return jnp.stack(outs, axis=0)
```

```python
import functools

import jax
import jax.numpy as jnp
from jax import lax
from jax.experimental import pallas as pl
from jax.experimental.pallas import tpu as pltpu

F32 = jnp.float32
BF16 = jnp.bfloat16

D_MODEL = 1024
D_FF = 4 * D_MODEL
EPS = 1e-6
N_MIXERS = 3

SWA_HEADS = 16
SWA_KV_HEADS = 4
SWA_HEAD_DIM = 64
SWA_WINDOW = 128

HGRN_HEADS = 8
HGRN_EXPAND = 128
HGRN_HEAD_DIM = 128
HGRN_CHUNK = 64
HGRN_SUB = 8
HGRN_STEP_CHUNKS = 2

FOX_HEADS = 16
FOX_HEAD_DIM = 64

LANES = 128
VMEM_LIMIT = 56 * 1024 * 1024
NEG_BIG = -1e30
LOG2E = 1.4426950408889634

ROW_TILE = 1024


def _cparams(sem):
    return pltpu.CompilerParams(dimension_semantics=sem, vmem_limit_bytes=VMEM_LIMIT)


def _const_spec(shape):
    nd = len(shape)
    return pl.BlockSpec(shape, lambda *_: (0,) * nd, pipeline_mode=pl.Buffered(1))


def _layer_spec(shape, layer):
    nd = len(shape)
    return pl.BlockSpec((None,) + tuple(shape), lambda *_: (layer,) + (0,) * nd, pipeline_mode=pl.Buffered(1))


def _rms(x, g):
    ms = jnp.mean(x * x, axis=-1, keepdims=True)
    return x * lax.rsqrt(ms + EPS) * g


def _norm_proj_kernel(x_ref, g_ref, w_ref, b_ref, *o_refs, segments, transposed, n_chunk):
    h = _rms(x_ref[...], g_ref[...]).astype(BF16)
    for o_ref, (lo, hi), tr in zip(o_refs, segments, transposed):
        for j in range(lo, hi, n_chunk):
            je = min(j + n_chunk, hi)
            y = jnp.dot(h, w_ref[:, j:je], preferred_element_type=F32) + b_ref[:, j:je]
            if tr:
                o_ref[j - lo:je - lo, :] = y.T.astype(o_ref.dtype)
            else:
                o_ref[:, j - lo:je - lo] = y.astype(o_ref.dtype)


def norm_proj(x, g, w, b, segments, dtypes, name, transposed=None):
    T, D = x.shape
    N = w.shape[1]
    tm = ROW_TILE
    transposed = tuple(transposed) if transposed is not None else (False,) * len(segments)
    out_shape, out_specs = [], []
    for (lo, hi), dt, tr in zip(segments, dtypes, transposed):
        if tr:
            out_shape.append(jax.ShapeDtypeStruct((hi - lo, T), dt))
            out_specs.append(pl.BlockSpec((hi - lo, tm), lambda i: (0, i)))
        else:
            out_shape.append(jax.ShapeDtypeStruct((T, hi - lo), dt))
            out_specs.append(pl.BlockSpec((tm, hi - lo), lambda i: (i, 0)))
    return pl.pallas_call(
        functools.partial(_norm_proj_kernel, segments=tuple(segments), transposed=transposed, n_chunk=512),
        grid=(T // tm,),
        in_specs=[
            pl.BlockSpec((tm, D), lambda i: (i, 0)),
            _const_spec((1, D)),
            _const_spec((D, N)),
            _const_spec((1, N)),
        ],
        out_specs=out_specs,
        out_shape=out_shape,
        compiler_params=_cparams(("parallel",)),
        name=name,
    )(x, g.reshape(1, D), w, b.reshape(1, N))


def _hgrn_gate(r_ref, gate_ref, gain_ref):
    parts = []
    for h in range(HGRN_HEADS):
        sl = slice(h * HGRN_HEAD_DIM, (h + 1) * HGRN_HEAD_DIM)
        r = r_ref[:, sl]
        r = r * lax.rsqrt(jnp.mean(r * r, axis=-1, keepdims=True) + EPS)
        g = gate_ref[:, sl]
        parts.append((r * gain_ref[:, sl] * (g * jax.nn.sigmoid(g))).astype(BF16))
    return jnp.concatenate(parts, axis=1)


def _mlp_tail(x, g_ref, wu_ref, wd_ref, gf_ref, o_ref, ff_chunk, final_norm):
    h = _rms(x, g_ref[...]).astype(BF16)
    acc = x
    for j in range(0, D_FF, ff_chunk):
        u = jnp.dot(h, wu_ref[:, j:j + ff_chunk], preferred_element_type=F32)
        u = jnp.maximum(u, 0.0)
        u = (u * u).astype(BF16)
        acc = acc + jnp.dot(u, wd_ref[j:j + ff_chunk, :], preferred_element_type=F32)
    if final_norm:
        acc = _rms(acc, gf_ref[...])
    o_ref[...] = acc


def _out_mlp_kernel(a_ref, wo_ref, x_ref, g_ref, wu_ref, wd_ref, gf_ref, o_ref, *, ff_chunk, final_norm):
    x = x_ref[...] + jnp.dot(a_ref[...], wo_ref[...], preferred_element_type=F32)
    _mlp_tail(x, g_ref, wu_ref, wd_ref, gf_ref, o_ref, ff_chunk, final_norm)


def _gated_out_mlp_kernel(r_ref, gate_ref, gain_ref, wo_ref, x_ref, g_ref, wu_ref, wd_ref, gf_ref, o_ref, *,
                          ff_chunk, final_norm):
    a = _hgrn_gate(r_ref, gate_ref, gain_ref)
    x = x_ref[...] + jnp.dot(a, wo_ref[...], preferred_element_type=F32)
    _mlp_tail(x, g_ref, wu_ref, wd_ref, gf_ref, o_ref, ff_chunk, final_norm)


def out_mlp(mixer_out, w_o, x, g, w_up_all, w_down_all, layer, g_final, final_norm):
    T, D = x.shape
    gated = isinstance(mixer_out, tuple)
    tm = ROW_TILE // 2 if gated else ROW_TILE
    row = pl.BlockSpec((tm, D), lambda i: (i, 0))
    if gated:
        r, gate, gain = mixer_out
        body, mix_args, mix_specs = _gated_out_mlp_kernel, (r, gate, gain.reshape(1, D)), [row, row, _const_spec((1, D))]
    else:
        body, mix_args, mix_specs = _out_mlp_kernel, (mixer_out,), [row]
    return pl.pallas_call(
        functools.partial(body, ff_chunk=512, final_norm=final_norm),
        grid=(T // tm,),
        in_specs=mix_specs + [
            _const_spec((D, D)),
            row,
            _const_spec((1, D)),
            _layer_spec((D, D_FF), layer),
            _layer_spec((D_FF, D), layer),
            _const_spec((1, D)),
        ],
        out_specs=row,
        out_shape=jax.ShapeDtypeStruct((T, D), F32),
        compiler_params=_cparams(("parallel",)),
        name="out_mlp",
    )(*mix_args, w_o, x, g.reshape(1, D), w_up_all, w_down_all, g_final.reshape(1, D))


SWA_VROWS = SWA_HEAD_DIM + 16
SWA_STEP_BLOCKS = 4


def _swa_kernel(sink_ref, qt_ref, kc_ref, kp_ref, vtc_ref, vtp_ref, o_ref):
    n = pl.program_id(0)
    W, dh, NB = SWA_WINDOW, SWA_HEAD_DIM, SWA_STEP_BLOCKS
    G = SWA_HEADS // SWA_KV_HEADS
    key = lax.broadcasted_iota(jnp.int32, (2 * W, G * W), 0)
    lane = lax.broadcasted_iota(jnp.int32, (2 * W, G * W), 1)
    rel = lane % W + W - key
    in_window = (rel >= 0) & (rel < W)
    group_lane = lax.broadcasted_iota(jnp.int32, (1, G * W), 1) // W
    k_all = jnp.concatenate([kp_ref[...], kc_ref[...]], axis=0)
    vt_all = jnp.concatenate([vtp_ref[...], vtc_ref[...]], axis=1)
    ones_rows = jnp.ones((SWA_VROWS - dh, 2 * W), BF16)
    units = [(b, hk) for b in range(NB) for hk in range(SWA_KV_HEADS)]

    sts = []
    for b, hk in units:
        pair, half = divmod(hk, 2)
        lhs = k_all[b * W:(b + 2) * W, pair * LANES:(pair + 1) * LANES]
        q_cat = jnp.concatenate([qt_ref[(hk * G + g) * dh:(hk * G + g + 1) * dh, b * W:(b + 1) * W]
                                 for g in range(G)], axis=1)
        zeros = jnp.zeros_like(q_cat)
        rhs = jnp.concatenate([q_cat, zeros] if half == 0 else [zeros, q_cat], axis=0)
        sts.append(jnp.dot(lhs, rhs, preferred_element_type=F32))

    pts, sink_terms = [], []
    for (b, hk), st in zip(units, sts):
        visible = in_window if b > 0 else in_window & ((n > 0) | (key >= W))
        st = jnp.where(visible, st, NEG_BIG)
        sink = jnp.zeros((1, G * W), F32)
        for g in range(G):
            sink = jnp.where(group_lane == g, sink_ref[hk * G + g] * LOG2E, sink)
        m = jnp.maximum(jnp.max(st, axis=0, keepdims=True), sink)
        pts.append(jnp.exp2(st - m).astype(BF16))
        sink_terms.append(jnp.exp2(sink - m))

    accs = []
    for (b, hk), pt in zip(units, pts):
        vt_aug = jnp.concatenate([vt_all[hk * dh:(hk + 1) * dh, b * W:(b + 2) * W], ones_rows], axis=0)
        accs.append(jnp.dot(vt_aug, pt, preferred_element_type=F32))

    outs = [[] for _ in range(NB)]
    for (b, hk), acc, sink_term in zip(units, accs, sink_terms):
        ot = acc[:dh] / (acc[dh:dh + 1] + sink_term)
        for g in range(0, G, 2):
            two = jnp.concatenate([ot[:, g * W:(g + 1) * W], ot[:, (g + 1) * W:(g + 2) * W]], axis=0)
            outs[b].append(two.T)
    for b in range(NB):
        o_ref[b * W:(b + 1) * W, :] = jnp.concatenate(outs[b], axis=1).astype(o_ref.dtype)


def swa_attention(qt, k, vt, sinks):
    T = k.shape[0]
    W, NB = SWA_WINDOW, SWA_STEP_BLOCKS
    nq = SWA_HEADS * SWA_HEAD_DIM
    kvw = SWA_KV_HEADS * SWA_HEAD_DIM
    prev = lambda n: jnp.maximum(NB * n - 1, 0)
    return pl.pallas_call(
        _swa_kernel,
        grid=(T // (NB * W),),
        in_specs=[
            pl.BlockSpec(memory_space=pltpu.SMEM),
            pl.BlockSpec((nq, NB * W), lambda n: (0, n)),
            pl.BlockSpec((NB * W, kvw), lambda n: (n, 0)),
            pl.BlockSpec((W, kvw), lambda n: (prev(n), 0)),
            pl.BlockSpec((kvw, NB * W), lambda n: (0, n)),
            pl.BlockSpec((kvw, W), lambda n: (0, prev(n))),
        ],
        out_specs=pl.BlockSpec((NB * W, nq), lambda n: (n, 0)),
        out_shape=jax.ShapeDtypeStruct((T, nq), BF16),
        compiler_params=_cparams(("parallel",)),
        name="swa",
    )(sinks, qt, k, k, vt, vt)


def _split2(x):
    hi = x.astype(BF16)
    lo = (x - hi.astype(F32)).astype(BF16)
    return hi, lo


def _split_dot(m_bf16, parts):
    hi, lo = parts
    return (jnp.dot(m_bf16, hi, preferred_element_type=F32)
            + jnp.dot(m_bf16, lo, preferred_element_type=F32))


def _hgrn_kernel(q_ref, f_ref, v_ref, lb_ref, o_ref, st_ref):
    C, SUB, K, H = HGRN_CHUNK, HGRN_SUB, HGRN_EXPAND, HGRN_HEADS
    R = C * HGRN_STEP_CHUNKS
    nt = (((1,), (1,)), ((), ()))
    tn = (((0,), (0,)), ((), ()))

    @pl.when(pl.program_id(0) == 0)
    def _():
        st_ref[...] = jnp.zeros_like(st_ref)

    row = lax.broadcasted_iota(jnp.int32, (R, R), 0)
    col = lax.broadcasted_iota(jnp.int32, (R, R), 1)
    same_chunk = row // C == col // C
    heads = [slice(h * K, (h + 1) * K) for h in range(H)]
    chunks = [slice(c * C, (c + 1) * C) for c in range(HGRN_STEP_CHUNKS)]

    lb = lb_ref[...]
    f = lb + (1.0 - lb) * jax.nn.sigmoid(f_ref[...])
    logf = _split2(jnp.log2(f))
    kk = 1.0 - f
    q = q_ref[...]
    qs = q * jax.nn.sigmoid(q)
    v = v_ref[...]
    bcum = _split_dot(((row >= col) & same_chunk).astype(BF16), logf)
    to_end = _split_dot(((row < col) & same_chunk).astype(BF16), logf)

    q_in = (qs * jnp.exp2(bcum)).astype(BF16)
    k_out = (kk * jnp.exp2(to_end)).astype(BF16)
    o_in = [[] for _ in heads]
    for cs in chunks:
        decay = jnp.exp2(bcum[cs.stop - 1:cs.stop, :])
        for h, hs in enumerate(heads):
            st = st_ref[h]
            o_in[h].append(lax.dot_general(q_in[cs, hs], st.astype(BF16), nt, preferred_element_type=F32))
            upd = lax.dot_general(v[cs, hs], k_out[cs, hs], tn, preferred_element_type=F32)
            st_ref[h] = st * decay[:, hs] + upd
    o = [jnp.concatenate(parts, axis=0) for parts in o_in]


    a = [jnp.zeros((R, R), F32) for _ in heads]
    half = SUB
    while half < C:
        pos = row % (2 * half)
        bnd = row - pos + half - 1
        to_bnd = (((pos >= half) & (col > bnd) & (col <= row))
                  | ((pos < half) & (col > row) & (col <= bnd)))
        e = jnp.exp2(_split_dot(to_bnd.astype(BF16), logf))
        q_e = (qs * e).astype(BF16)
        k_e = (kk * e).astype(BF16)
        right_left = (pos >= half) & (col <= bnd) & (col > bnd - half)
        for h, hs in enumerate(heads):
            a_l = lax.dot_general(q_e[:, hs], k_e[:, hs], nt, preferred_element_type=F32)
            a[h] = jnp.where(right_left, a_l, a[h])
        half *= 2

    sub_pos = lax.broadcasted_iota(jnp.int32, (R, H * K), 0) % SUB
    for d in range(SUB):
        if d == 0:
            e = qs * kk
        else:
            e = qs * pltpu.roll(kk, d, 0) * jnp.exp2(bcum - pltpu.roll(bcum, d, 0))
            e = jnp.where(sub_pos >= d, e, 0.0)
        band = (row - col == d) & (row % SUB >= d)
        for h, hs in enumerate(heads):
            a[h] = jnp.where(band, jnp.sum(e[:, hs], axis=1, keepdims=True), a[h])

    for h, hs in enumerate(heads):
        o[h] = o[h] + jnp.dot(a[h].astype(BF16), v[:, hs], preferred_element_type=F32)
    o_ref[...] = jnp.concatenate(o, axis=1)


def hgrn_recurrence(q, f_logit, v, lb):
    T = q.shape[0]
    rows = HGRN_CHUNK * HGRN_STEP_CHUNKS
    width = HGRN_HEADS * HGRN_EXPAND
    blk = pl.BlockSpec((rows, width), lambda c: (c, 0))
    return pl.pallas_call(
        _hgrn_kernel,
        grid=(T // rows,),
        in_specs=[blk, blk, blk, _const_spec((1, width))],
        out_specs=blk,
        out_shape=jax.ShapeDtypeStruct((T, width), F32),
        scratch_shapes=[pltpu.VMEM((HGRN_HEADS, HGRN_HEAD_DIM, HGRN_EXPAND), F32)],
        compiler_params=_cparams(("arbitrary",)),
        name="hgrn",
    )(q, f_logit, v, lb.reshape(1, -1))


FOX_CUM_TILE = 512


def _split3(x):
    hi = x.astype(BF16)
    r1 = x - hi.astype(F32)
    mid = r1.astype(BF16)
    lo = (r1 - mid.astype(F32)).astype(BF16)
    return hi, mid, lo


FOX_ONES_ROW = 3 * FOX_HEADS


def _fox_cumsum_kernel(f_ref, kaug_ref, ct_ref, carry_ref):
    @pl.when(pl.program_id(0) == 0)
    def _():
        carry_ref[...] = jnp.zeros_like(carry_ref)

    x = f_ref[...]
    ls = jnp.minimum(x, 0.0) - jnp.log(1.0 + jnp.exp(-jnp.abs(x)))
    n = FOX_CUM_TILE
    tril = (lax.broadcasted_iota(jnp.int32, (n, n), 0) >= lax.broadcasted_iota(jnp.int32, (n, n), 1)).astype(BF16)
    hi, mid, lo = _split3(ls)
    cum = (jnp.dot(tril, hi, preferred_element_type=F32)
           + jnp.dot(tril, mid, preferred_element_type=F32)
           + jnp.dot(tril, lo, preferred_element_type=F32))
    c = cum + carry_ref[...]
    carry_ref[...] = c[n - 1:n, :]
    H = FOX_HEADS
    hi2, mid2, lo2 = (a.astype(F32) for a in _split3(c * LOG2E))
    lane = lax.broadcasted_iota(jnp.int32, (n, LANES), 1)
    packed = jnp.where(lane < H, hi2,
                       jnp.where(lane < 2 * H, pltpu.roll(mid2, H, 1),
                                 jnp.where(lane < 3 * H, pltpu.roll(lo2, 2 * H, 1), 0.0)))
    is_one = (lane >= FOX_ONES_ROW) & (lane < FOX_ONES_ROW + 3)
    kaug_ref[...] = jnp.where(is_one, 1.0, -packed).astype(BF16)
    ct_ref[...] = packed.T


def fox_cumsum(f_logit):
    T = f_logit.shape[0]
    n = FOX_CUM_TILE
    spec = pl.BlockSpec((n, LANES), lambda i: (i, 0))
    return pl.pallas_call(
        _fox_cumsum_kernel,
        grid=(T // n,),
        in_specs=[spec],
        out_specs=[spec, pl.BlockSpec((LANES, n), lambda i: (0, i))],
        out_shape=[jax.ShapeDtypeStruct((T, LANES), BF16), jax.ShapeDtypeStruct((LANES, T), F32)],
        scratch_shapes=[pltpu.VMEM((1, LANES), F32)],
        compiler_params=_cparams(("arbitrary",)),
        name="fox_cumsum",
    )(f_logit)


FOX_TILE = 512
FOX_KEY_TILE = 256
FOX_STEP_HEADS = 2
FOX_STEP_TILES = 2


FOX_VROWS = FOX_HEAD_DIM + 16


def _fox_kernel(first_ref, k_ref, kaug_ref, qt_ref, ct_ref, vt_ref, o_ref, *scratch):
    t = FOX_TILE
    for u in range(FOX_STEP_TILES):
        cols = slice(u * t, (u + 1) * t)
        _fox_tile(FOX_STEP_TILES * pl.program_id(1) + u, first_ref, k_ref, kaug_ref, qt_ref.at[:, cols],
                  ct_ref.at[:, cols], vt_ref, o_ref.at[cols, :], *(s.at[u] for s in scratch))


def _fox_tile(i, first_ref, k_ref, kaug_ref, qt_ref, ct_ref, vt_ref, o_ref,
              rhs_ref, acc_ref, m_ref, alpha_ref, st_ref, pt_ref):
    p = pl.program_id(0)
    tq, tk = FOX_TILE, FOX_KEY_TILE
    dh = FOX_HEAD_DIM
    H = FOX_HEADS

    NH = FOX_STEP_HEADS
    row = lax.broadcasted_iota(jnp.int32, (LANES, tq), 0)
    for h in range(NH):
        head = NH * p + h
        pair, half = divmod(h, 2)
        qt = qt_ref[pair * LANES:(pair + 1) * LANES, :].astype(F32)
        q_part = jnp.where((row >= half * dh) & (row < (half + 1) * dh), qt, 0.0)
        gate_rows = (row == head) | (row == H + head) | (row == 2 * H + head)
        c_parts = [ct_ref[pl.ds(part * H + head, 1), :] for part in range(3)]
        aug = jnp.where(gate_rows, 1.0, 0.0)
        for part in range(3):
            aug = jnp.where(row == FOX_ONES_ROW + part, c_parts[part], aug)
        rhs_ref[h] = jnp.concatenate([q_part, aug], axis=0).astype(BF16)

    acc_ref[...] = jnp.zeros_like(acc_ref)
    m_ref[...] = jnp.full_like(m_ref, NEG_BIG)
    pt_ref[1] = jnp.zeros(pt_ref.shape[1:], BF16)
    alpha_ref[1] = jnp.ones(alpha_ref.shape[1:], F32)

    def logits(j, slot):
        start = pl.multiple_of(j * tk, tk)
        gates = kaug_ref[pl.ds(start, tk), :]
        for h in range(NH):
            pair = h // 2
            lhs = jnp.concatenate([k_ref[pl.ds(start, tk), pair * LANES:(pair + 1) * LANES], gates], axis=1)
            st_ref[slot, h] = jnp.dot(lhs, rhs_ref[h], preferred_element_type=F32)

    def softmax(slot, mask_offset=None):
        for h in range(NH):
            st = st_ref[slot, h]
            if mask_offset is not None:
                causal = (lax.broadcasted_iota(jnp.int32, (tk, tq), 0) + mask_offset
                          <= lax.broadcasted_iota(jnp.int32, (tk, tq), 1))
                st = jnp.where(causal, st, NEG_BIG)
            m_old = m_ref[h]
            m_new = jnp.maximum(m_old, jnp.max(st, axis=0, keepdims=True))
            alpha_ref[slot, h] = jnp.exp2(m_old - m_new)
            pt_ref[slot, h] = jnp.exp2(st - m_new).astype(BF16)
            m_ref[h] = m_new

    ones_rows = jnp.ones((FOX_VROWS - dh, tk), BF16)

    def accum(j, slot):
        start = pl.multiple_of(j * tk, tk)
        for h in range(NH):
            vt_aug = jnp.concatenate([vt_ref[h * dh:(h + 1) * dh, pl.ds(start, tk)], ones_rows], axis=0)
            pv = jnp.dot(vt_aug, pt_ref[slot, h], preferred_element_type=F32)
            acc_ref[h] = alpha_ref[slot, h] * acc_ref[h] + pv

    sub = tq // tk

    def body(p, carry):
        accum(jnp.maximum(sub * p - 1, 0), 1)
        softmax(0)
        logits(sub * p + 1, 1)
        accum(sub * p, 0)
        softmax(1)
        logits(sub * p + 2, 0)
        return carry

    first = first_ref[p, i]
    logits(sub * first, 0)
    lax.fori_loop(first, i, body, 0)
    logits(sub * i + 1, 1)
    accum(jnp.maximum(sub * i - 1, 0), 1)
    softmax(0, mask_offset=0)
    accum(sub * i, 0)
    softmax(1, mask_offset=tk)
    accum(sub * i + 1, 1)

    for pair in range(NH // 2):
        acc_a, acc_b = acc_ref[2 * pair], acc_ref[2 * pair + 1]
        ot = jnp.concatenate([acc_a[:dh] / acc_a[dh:dh + 1], acc_b[:dh] / acc_b[dh:dh + 1]], axis=0)
        o_ref[:, pair * LANES:(pair + 1) * LANES] = ot.T.astype(o_ref.dtype)


def _fox_norms_kernel(k_ref, qt_ref, kn_ref, qn_ref):
    n = FOX_HEADS * FOX_HEAD_DIM
    chan = lax.broadcasted_iota(jnp.int32, (n, LANES), 0) // FOX_HEAD_DIM
    head = lax.broadcasted_iota(jnp.int32, (n, LANES), 1)
    group = (chan == head).astype(BF16)
    k = k_ref[...].astype(F32)
    k2 = jnp.dot((k * k).astype(BF16), group, preferred_element_type=F32)
    q = qt_ref[...].astype(F32)
    q2 = lax.dot_general((q * q).astype(BF16), group, (((0,), (0,)), ((), ())), preferred_element_type=F32)
    kn_ref[0] = jnp.broadcast_to(jnp.max(k2, axis=0, keepdims=True), kn_ref.shape[1:])
    qn_ref[0] = jnp.broadcast_to(jnp.max(q2, axis=0, keepdims=True), qn_ref.shape[1:])


def fox_norms(k, qt):
    T = k.shape[0]
    t = FOX_TILE
    n = FOX_HEADS * FOX_HEAD_DIM
    out = pl.BlockSpec((1, 8, LANES), lambda i: (i, 0, 0))
    return pl.pallas_call(
        _fox_norms_kernel,
        grid=(T // t,),
        in_specs=[pl.BlockSpec((t, n), lambda i: (i, 0)), pl.BlockSpec((n, t), lambda i: (0, i))],
        out_specs=[out, out],
        out_shape=[jax.ShapeDtypeStruct((T // t, 8, LANES), F32)] * 2,
        compiler_params=_cparams(("parallel",)),
        name="fox_norms",
    )(k, qt)


FOX_UNDERFLOW_LOG2 = 160.0


def fox_first_block(kn2, qn2, ct):
    H = FOX_HEADS
    nt = kn2.shape[0]
    bk = jnp.sqrt(jnp.max(kn2[:, 0, :H], axis=0))
    bq = jnp.sqrt(qn2[:, 0, :H])
    need = 2.0 * 1.02 * bq * bk[None, :] + FOX_UNDERFLOW_LOG2 + 2.0
    c_first = ct[:, 0::FOX_TILE]
    c_last = ct[:, FOX_TILE - 1::FOX_TILE]
    c_first = c_first[:H] + c_first[H:2 * H] + c_first[2 * H:3 * H]
    c_last = c_last[:H] + c_last[H:2 * H] + c_last[2 * H:3 * H]
    gap = c_last[:, None, :] - c_first[:, :, None]
    earlier = jnp.arange(nt)[None, :] < jnp.arange(nt)[:, None]
    prunable = (gap > need.T[:, :, None]) & earlier[None]
    n_skip = jnp.min(jnp.where(prunable, nt, jnp.arange(nt)[None, None, :]), axis=2)
    return jnp.min(n_skip.reshape(H // FOX_STEP_HEADS, FOX_STEP_HEADS, nt), axis=1).astype(jnp.int32)


def fox_attention(first, k, kaug, qt, ct, vt):
    T = k.shape[0]
    t = FOX_TILE
    nt = FOX_STEP_TILES
    nh = FOX_STEP_HEADS
    width = nh * FOX_HEAD_DIM
    return pl.pallas_call(
        _fox_kernel,
        grid=(FOX_HEADS // nh, T // (nt * t)),
        in_specs=[
            pl.BlockSpec(memory_space=pltpu.SMEM),
            pl.BlockSpec((T, width), lambda p, i: (0, p)),
            pl.BlockSpec((T, LANES), lambda p, i: (0, 0), pipeline_mode=pl.Buffered(1)),
            pl.BlockSpec((width, nt * t), lambda p, i: (p, i)),
            pl.BlockSpec((LANES, nt * t), lambda p, i: (0, i)),
            pl.BlockSpec((width, T), lambda p, i: (p, 0)),
        ],
        out_specs=pl.BlockSpec((nt * t, width), lambda p, i: (i, p)),
        out_shape=jax.ShapeDtypeStruct((T, FOX_HEADS * FOX_HEAD_DIM), BF16),
        scratch_shapes=[
            pltpu.VMEM((nt, nh, 2 * LANES, t), BF16),
            pltpu.VMEM((nt, nh, FOX_VROWS, t), F32),
            pltpu.VMEM((nt, nh, 1, t), F32),
            pltpu.VMEM((nt, 2, nh, 1, t), F32),
            pltpu.VMEM((nt, 2, nh, FOX_KEY_TILE, t), F32),
            pltpu.VMEM((nt, 2, nh, FOX_KEY_TILE, t), BF16),
        ],
        compiler_params=_cparams(("parallel", "arbitrary")),
        name="fox",
    )(first, k, kaug, qt, ct, vt)


def kernel(x, norm_mix, norm_mlp, norm_final, w_up, w_down, swa_w_qkv, swa_b_qkv, swa_sinks, swa_w_o,
           hgrn_w_in, hgrn_lb_logits, hgrn_g_norm, hgrn_w_o, fox_w_in, fox_b_in, fox_w_o):
    B, T, D = x.shape
    depth = norm_mix.shape[0]
    lb_soft = jax.nn.softmax(hgrn_lb_logits.astype(F32), axis=0)
    lower_bounds = jnp.cumsum(lb_soft, axis=0) - lb_soft[0]
    w_up_bf16 = w_up.astype(BF16)
    w_down_bf16 = w_down.astype(BF16)

    outs = []
    for b in range(B):
        xb = x[b]
        for i in range(depth):
            m, j = i % N_MIXERS, i // N_MIXERS
            if m == 0:
                nq = SWA_HEADS * SWA_HEAD_DIM
                nkv = SWA_KV_HEADS * SWA_HEAD_DIM
                col_scale = jnp.concatenate([jnp.full((nq,), LOG2E * SWA_HEAD_DIM ** -0.5, F32),
                                             jnp.ones((2 * nkv,), F32)])
                qt, k, vt = norm_proj(xb, norm_mix[i], (swa_w_qkv[j] * col_scale).astype(BF16),
                                      swa_b_qkv[j] * col_scale,
                                      [(0, nq), (nq, nq + nkv), (nq + nkv, nq + 2 * nkv)], [BF16] * 3, "swa_proj",
                                      transposed=(True, False, True))
                a = swa_attention(qt, k, vt, swa_sinks[j])
                w_o = swa_w_o[j]
            elif m == 1:
                hk = HGRN_HEADS * HGRN_EXPAND
                hv = HGRN_HEADS * HGRN_HEAD_DIM
                q, f, v, g = norm_proj(xb, norm_mix[i], hgrn_w_in[j].astype(BF16), jnp.zeros((2 * hk + 2 * hv,), F32),
                                       [(0, hk), (hk, 2 * hk), (2 * hk, 2 * hk + hv), (2 * hk + hv, 2 * hk + 2 * hv)],
                                       [F32, F32, BF16, F32], "hgrn_proj")
                o = hgrn_recurrence(q, f, v, lower_bounds[i])
                a = (o, g, hgrn_g_norm[j])
                w_o = hgrn_w_o[j]
            else:
                n = FOX_HEADS * FOX_HEAD_DIM
                pad = LANES - FOX_HEADS
                col_scale = jnp.concatenate([jnp.full((n,), LOG2E * FOX_HEAD_DIM ** -0.5, F32),
                                             jnp.ones((2 * n + FOX_HEADS,), F32)])
                w = jnp.pad(fox_w_in[j] * col_scale, ((0, 0), (0, pad))).astype(BF16)
                bias = jnp.pad(fox_b_in[j] * col_scale, ((0, pad),))
                qt, k, vt, f = norm_proj(xb, norm_mix[i], w, bias,
                                         [(0, n), (n, 2 * n), (2 * n, 3 * n), (3 * n, 3 * n + LANES)],
                                         [BF16, BF16, BF16, F32], "fox_proj",
                                         transposed=(True, False, True, False))
                kaug, ct = fox_cumsum(f)
                first = fox_first_block(*fox_norms(k, qt), ct)
                a = fox_attention(first, k, kaug, qt, ct, vt)
                w_o = fox_w_o[j]
            last = i == depth - 1
            xb = out_mlp(a, w_o.astype(BF16), xb, norm_mlp[i], w_up_bf16, w_down_bf16, i, norm_final, last)
        outs.append(xb)
    return jnp.stack(outs, axis=0)
```

```python
import functools

import jax
import jax.numpy as jnp
from jax import lax
from jax.experimental import pallas as pl
from jax.experimental.pallas import tpu as pltpu

F32 = jnp.float32
BF16 = jnp.bfloat16

D_MODEL = 1024
D_FF = 4 * D_MODEL
EPS = 1e-6
N_MIXERS = 3

SWA_HEADS = 16
SWA_KV_HEADS = 4
SWA_HEAD_DIM = 64
SWA_WINDOW = 128

HGRN_HEADS = 8
HGRN_EXPAND = 128
HGRN_HEAD_DIM = 128
HGRN_CHUNK = 64
HGRN_SUB = 8
HGRN_STEP_CHUNKS = 2

FOX_HEADS = 16
FOX_HEAD_DIM = 64

LANES = 128
VMEM_LIMIT = 56 * 1024 * 1024
NEG_BIG = -1e30
LOG2E = 1.4426950408889634

ROW_TILE = 1024


def _cparams(sem):
    return pltpu.CompilerParams(dimension_semantics=sem, vmem_limit_bytes=VMEM_LIMIT)


def _const_spec(shape):
    nd = len(shape)
    return pl.BlockSpec(shape, lambda *_: (0,) * nd, pipeline_mode=pl.Buffered(1))


def _layer_spec(shape, layer):
    nd = len(shape)
    return pl.BlockSpec((None,) + tuple(shape), lambda *_: (layer,) + (0,) * nd, pipeline_mode=pl.Buffered(1))


def _rms(x, g):
    ms = jnp.mean(x * x, axis=-1, keepdims=True)
    return x * lax.rsqrt(ms + EPS) * g


def _norm_proj_kernel(x_ref, g_ref, w_ref, b_ref, *o_refs, segments, transposed, n_chunk):
    h = _rms(x_ref[...], g_ref[...]).astype(BF16)
    for o_ref, (lo, hi), tr in zip(o_refs, segments, transposed):
        for j in range(lo, hi, n_chunk):
            je = min(j + n_chunk, hi)
            y = jnp.dot(h, w_ref[:, j:je], preferred_element_type=F32) + b_ref[:, j:je]
            if tr:
                o_ref[j - lo:je - lo, :] = y.T.astype(o_ref.dtype)
            else:
                o_ref[:, j - lo:je - lo] = y.astype(o_ref.dtype)


def norm_proj(x, g, w, b, segments, dtypes, name, transposed=None):
    T, D = x.shape
    N = w.shape[1]
    tm = ROW_TILE
    transposed = tuple(transposed) if transposed is not None else (False,) * len(segments)
    out_shape, out_specs = [], []
    for (lo, hi), dt, tr in zip(segments, dtypes, transposed):
        if tr:
            out_shape.append(jax.ShapeDtypeStruct((hi - lo, T), dt))
            out_specs.append(pl.BlockSpec((hi - lo, tm), lambda i: (0, i)))
        else:
            out_shape.append(jax.ShapeDtypeStruct((T, hi - lo), dt))
            out_specs.append(pl.BlockSpec((tm, hi - lo), lambda i: (i, 0)))
    return pl.pallas_call(
        functools.partial(_norm_proj_kernel, segments=tuple(segments), transposed=transposed, n_chunk=512),
        grid=(T // tm,),
        in_specs=[
            pl.BlockSpec((tm, D), lambda i: (i, 0)),
            _const_spec((1, D)),
            _const_spec((D, N)),
            _const_spec((1, N)),
        ],
        out_specs=out_specs,
        out_shape=out_shape,
        compiler_params=_cparams(("parallel",)),
        name=name,
    )(x, g.reshape(1, D), w, b.reshape(1, N))


def _hgrn_gate(r_ref, gate_ref, gain_ref):
    parts = []
    for h in range(HGRN_HEADS):
        sl = slice(h * HGRN_HEAD_DIM, (h + 1) * HGRN_HEAD_DIM)
        r = r_ref[:, sl]
        r = r * lax.rsqrt(jnp.mean(r * r, axis=-1, keepdims=True) + EPS)
        g = gate_ref[:, sl]
        parts.append((r * gain_ref[:, sl] * (g * jax.nn.sigmoid(g))).astype(BF16))
    return jnp.concatenate(parts, axis=1)


def _mlp_tail(x, g_ref, wu_ref, wd_ref, gf_ref, o_ref, ff_chunk, final_norm):
    h = _rms(x, g_ref[...]).astype(BF16)
    acc = x
    for j in range(0, D_FF, ff_chunk):
        u = jnp.dot(h, wu_ref[:, j:j + ff_chunk], preferred_element_type=F32)
        u = jnp.maximum(u, 0.0)
        u = (u * u).astype(BF16)
        acc = acc + jnp.dot(u, wd_ref[j:j + ff_chunk, :], preferred_element_type=F32)
    if final_norm:
        acc = _rms(acc, gf_ref[...])
    o_ref[...] = acc


def _out_mlp_kernel(a_ref, wo_ref, x_ref, g_ref, wu_ref, wd_ref, gf_ref, o_ref, *, ff_chunk, final_norm):
    x = x_ref[...] + jnp.dot(a_ref[...], wo_ref[...], preferred_element_type=F32)
    _mlp_tail(x, g_ref, wu_ref, wd_ref, gf_ref, o_ref, ff_chunk, final_norm)


def _gated_out_mlp_kernel(r_ref, gate_ref, gain_ref, wo_ref, x_ref, g_ref, wu_ref, wd_ref, gf_ref, o_ref, *,
                          ff_chunk, final_norm):
    a = _hgrn_gate(r_ref, gate_ref, gain_ref)
    x = x_ref[...] + jnp.dot(a, wo_ref[...], preferred_element_type=F32)
    _mlp_tail(x, g_ref, wu_ref, wd_ref, gf_ref, o_ref, ff_chunk, final_norm)


def out_mlp(mixer_out, w_o, x, g, w_up_all, w_down_all, layer, g_final, final_norm):
    T, D = x.shape
    gated = isinstance(mixer_out, tuple)
    tm = ROW_TILE // 2 if gated else ROW_TILE
    row = pl.BlockSpec((tm, D), lambda i: (i, 0))
    if gated:
        r, gate, gain = mixer_out
        body, mix_args, mix_specs = _gated_out_mlp_kernel, (r, gate, gain.reshape(1, D)), [row, row, _const_spec((1, D))]
    else:
        body, mix_args, mix_specs = _out_mlp_kernel, (mixer_out,), [row]
    return pl.pallas_call(
        functools.partial(body, ff_chunk=512, final_norm=final_norm),
        grid=(T // tm,),
        in_specs=mix_specs + [
            _const_spec((D, D)),
            row,
            _const_spec((1, D)),
            _layer_spec((D, D_FF), layer),
            _layer_spec((D_FF, D), layer),
            _const_spec((1, D)),
        ],
        out_specs=row,
        out_shape=jax.ShapeDtypeStruct((T, D), F32),
        compiler_params=_cparams(("parallel",)),
        name="out_mlp",
    )(*mix_args, w_o, x, g.reshape(1, D), w_up_all, w_down_all, g_final.reshape(1, D))


SWA_VROWS = SWA_HEAD_DIM + 16
SWA_STEP_BLOCKS = 4


def _swa_kernel(sink_ref, qt_ref, kc_ref, kp_ref, vtc_ref, vtp_ref, o_ref):
    n = pl.program_id(0)
    W, dh, NB = SWA_WINDOW, SWA_HEAD_DIM, SWA_STEP_BLOCKS
    G = SWA_HEADS // SWA_KV_HEADS
    key = lax.broadcasted_iota(jnp.int32, (2 * W, G * W), 0)
    lane = lax.broadcasted_iota(jnp.int32, (2 * W, G * W), 1)
    rel = lane % W + W - key
    in_window = (rel >= 0) & (rel < W)
    group_lane = lax.broadcasted_iota(jnp.int32, (1, G * W), 1) // W
    k_all = jnp.concatenate([kp_ref[...], kc_ref[...]], axis=0)
    vt_all = jnp.concatenate([vtp_ref[...], vtc_ref[...]], axis=1)
    ones_rows = jnp.ones((SWA_VROWS - dh, 2 * W), BF16)
    units = [(b, hk) for b in range(NB) for hk in range(SWA_KV_HEADS)]

    sts = []
    for b, hk in units:
        pair, half = divmod(hk, 2)
        lhs = k_all[b * W:(b + 2) * W, pair * LANES:(pair + 1) * LANES]
        q_cat = jnp.concatenate([qt_ref[(hk * G + g) * dh:(hk * G + g + 1) * dh, b * W:(b + 1) * W]
                                 for g in range(G)], axis=1)
        zeros = jnp.zeros_like(q_cat)
        rhs = jnp.concatenate([q_cat, zeros] if half == 0 else [zeros, q_cat], axis=0)
        sts.append(jnp.dot(lhs, rhs, preferred_element_type=F32))

    pts, sink_terms = [], []
    for (b, hk), st in zip(units, sts):
        visible = in_window if b > 0 else in_window & ((n > 0) | (key >= W))
        st = jnp.where(visible, st, NEG_BIG)
        sink = jnp.zeros((1, G * W), F32)
        for g in range(G):
            sink = jnp.where(group_lane == g, sink_ref[hk * G + g] * LOG2E, sink)
        m = jnp.maximum(jnp.max(st, axis=0, keepdims=True), sink)
        pts.append(jnp.exp2(st - m).astype(BF16))
        sink_terms.append(jnp.exp2(sink - m))

    accs = []
    for (b, hk), pt in zip(units, pts):
        vt_aug = jnp.concatenate([vt_all[hk * dh:(hk + 1) * dh, b * W:(b + 2) * W], ones_rows], axis=0)
        accs.append(jnp.dot(vt_aug, pt, preferred_element_type=F32))

    outs = [[] for _ in range(NB)]
    for (b, hk), acc, sink_term in zip(units, accs, sink_terms):
        ot = acc[:dh] / (acc[dh:dh + 1] + sink_term)
        for g in range(0, G, 2):
            two = jnp.concatenate([ot[:, g * W:(g + 1) * W], ot[:, (g + 1) * W:(g + 2) * W]], axis=0)
            outs[b].append(two.T)
    for b in range(NB):
        o_ref[b * W:(b + 1) * W, :] = jnp.concatenate(outs[b], axis=1).astype(o_ref.dtype)


def swa_attention(qt, k, vt, sinks):
    T = k.shape[0]
    W, NB = SWA_WINDOW, SWA_STEP_BLOCKS
    nq = SWA_HEADS * SWA_HEAD_DIM
    kvw = SWA_KV_HEADS * SWA_HEAD_DIM
    prev = lambda n: jnp.maximum(NB * n - 1, 0)
    return pl.pallas_call(
        _swa_kernel,
        grid=(T // (NB * W),),
        in_specs=[
            pl.BlockSpec(memory_space=pltpu.SMEM),
            pl.BlockSpec((nq, NB * W), lambda n: (0, n)),
            pl.BlockSpec((NB * W, kvw), lambda n: (n, 0)),
            pl.BlockSpec((W, kvw), lambda n: (prev(n), 0)),
            pl.BlockSpec((kvw, NB * W), lambda n: (0, n)),
            pl.BlockSpec((kvw, W), lambda n: (0, prev(n))),
        ],
        out_specs=pl.BlockSpec((NB * W, nq), lambda n: (n, 0)),
        out_shape=jax.ShapeDtypeStruct((T, nq), BF16),
        compiler_params=_cparams(("parallel",)),
        name="swa",
    )(sinks, qt, k, k, vt, vt)


def _split2(x):
    hi = x.astype(BF16)
    lo = (x - hi.astype(F32)).astype(BF16)
    return hi, lo


def _split_dot(m_bf16, parts):
    hi, lo = parts
    return (jnp.dot(m_bf16, hi, preferred_element_type=F32)
            + jnp.dot(m_bf16, lo, preferred_element_type=F32))


def _hgrn_kernel(q_ref, f_ref, v_ref, lb_ref, o_ref, st_ref):
    C, SUB, K, H = HGRN_CHUNK, HGRN_SUB, HGRN_EXPAND, HGRN_HEADS
    R = C * HGRN_STEP_CHUNKS
    nt = (((1,), (1,)), ((), ()))
    tn = (((0,), (0,)), ((), ()))

    @pl.when(pl.program_id(0) == 0)
    def _():
        st_ref[...] = jnp.zeros_like(st_ref)

    row = lax.broadcasted_iota(jnp.int32, (R, R), 0)
    col = lax.broadcasted_iota(jnp.int32, (R, R), 1)
    same_chunk = row // C == col // C
    heads = [slice(h * K, (h + 1) * K) for h in range(H)]
    chunks = [slice(c * C, (c + 1) * C) for c in range(HGRN_STEP_CHUNKS)]

    lb = lb_ref[...]
    f = lb + (1.0 - lb) * jax.nn.sigmoid(f_ref[...])
    logf = _split2(jnp.log2(f))
    kk = 1.0 - f
    q = q_ref[...]
    qs = q * jax.nn.sigmoid(q)
    v = v_ref[...]
    bcum = _split_dot(((row >= col) & same_chunk).astype(BF16), logf)
    to_end = _split_dot(((row < col) & same_chunk).astype(BF16), logf)

    q_in = (qs * jnp.exp2(bcum)).astype(BF16)
    k_out = (kk * jnp.exp2(to_end)).astype(BF16)
    o_in = [[] for _ in heads]
    for cs in chunks:
        decay = jnp.exp2(bcum[cs.stop - 1:cs.stop, :])
        for h, hs in enumerate(heads):
            st = st_ref[h]
            o_in[h].append(lax.dot_general(q_in[cs, hs], st.astype(BF16), nt, preferred_element_type=F32))
            upd = lax.dot_general(v[cs, hs], k_out[cs, hs], tn, preferred_element_type=F32)
            st_ref[h] = st * decay[:, hs] + upd
    o = [jnp.concatenate(parts, axis=0) for parts in o_in]


    a = [jnp.zeros((R, R), F32) for _ in heads]
    half = SUB
    while half < C:
        pos = row % (2 * half)
        bnd = row - pos + half - 1
        to_bnd = (((pos >= half) & (col > bnd) & (col <= row))
                  | ((pos < half) & (col > row) & (col <= bnd)))
        e = jnp.exp2(_split_dot(to_bnd.astype(BF16), logf))
        q_e = (qs * e).astype(BF16)
        k_e = (kk * e).astype(BF16)
        right_left = (pos >= half) & (col <= bnd) & (col > bnd - half)
        for h, hs in enumerate(heads):
            a_l = lax.dot_general(q_e[:, hs], k_e[:, hs], nt, preferred_element_type=F32)
            a[h] = jnp.where(right_left, a_l, a[h])
        half *= 2

    sub_pos = lax.broadcasted_iota(jnp.int32, (R, H * K), 0) % SUB
    for d in range(SUB):
        if d == 0:
            e = qs * kk
        else:
            e = qs * pltpu.roll(kk, d, 0) * jnp.exp2(bcum - pltpu.roll(bcum, d, 0))
            e = jnp.where(sub_pos >= d, e, 0.0)
        band = (row - col == d) & (row % SUB >= d)
        for h, hs in enumerate(heads):
            a[h] = jnp.where(band, jnp.sum(e[:, hs], axis=1, keepdims=True), a[h])

    for h, hs in enumerate(heads):
        o[h] = o[h] + jnp.dot(a[h].astype(BF16), v[:, hs], preferred_element_type=F32)
    o_ref[...] = jnp.concatenate(o, axis=1)


def hgrn_recurrence(q, f_logit, v, lb):
    T = q.shape[0]
    rows = HGRN_CHUNK * HGRN_STEP_CHUNKS
    width = HGRN_HEADS * HGRN_EXPAND
    blk = pl.BlockSpec((rows, width), lambda c: (c, 0))
    return pl.pallas_call(
        _hgrn_kernel,
        grid=(T // rows,),
        in_specs=[blk, blk, blk, _const_spec((1, width))],
        out_specs=blk,
        out_shape=jax.ShapeDtypeStruct((T, width), F32),
        scratch_shapes=[pltpu.VMEM((HGRN_HEADS, HGRN_HEAD_DIM, HGRN_EXPAND), F32)],
        compiler_params=_cparams(("arbitrary",)),
        name="hgrn",
    )(q, f_logit, v, lb.reshape(1, -1))


FOX_CUM_TILE = 512


def _split3(x):
    hi = x.astype(BF16)
    r1 = x - hi.astype(F32)
    mid = r1.astype(BF16)
    lo = (r1 - mid.astype(F32)).astype(BF16)
    return hi, mid, lo


FOX_ONES_ROW = 3 * FOX_HEADS


def _fox_cumsum_kernel(f_ref, kaug_ref, ct_ref, carry_ref):
    @pl.when(pl.program_id(0) == 0)
    def _():
        carry_ref[...] = jnp.zeros_like(carry_ref)

    x = f_ref[...]
    ls = jnp.minimum(x, 0.0) - jnp.log(1.0 + jnp.exp(-jnp.abs(x)))
    n = FOX_CUM_TILE
    tril = (lax.broadcasted_iota(jnp.int32, (n, n), 0) >= lax.broadcasted_iota(jnp.int32, (n, n), 1)).astype(BF16)
    hi, mid, lo = _split3(ls)
    cum = (jnp.dot(tril, hi, preferred_element_type=F32)
           + jnp.dot(tril, mid, preferred_element_type=F32)
           + jnp.dot(tril, lo, preferred_element_type=F32))
    c = cum + carry_ref[...]
    carry_ref[...] = c[n - 1:n, :]
    H = FOX_HEADS
    hi2, mid2, lo2 = (a.astype(F32) for a in _split3(c * LOG2E))
    lane = lax.broadcasted_iota(jnp.int32, (n, LANES), 1)
    packed = jnp.where(lane < H, hi2,
                       jnp.where(lane < 2 * H, pltpu.roll(mid2, H, 1),
                                 jnp.where(lane < 3 * H, pltpu.roll(lo2, 2 * H, 1), 0.0)))
    is_one = (lane >= FOX_ONES_ROW) & (lane < FOX_ONES_ROW + 3)
    kaug_ref[...] = jnp.where(is_one, 1.0, -packed).astype(BF16)
    ct_ref[...] = packed.T


def fox_cumsum(f_logit):
    T = f_logit.shape[0]
    n = FOX_CUM_TILE
    spec = pl.BlockSpec((n, LANES), lambda i: (i, 0))
    return pl.pallas_call(
        _fox_cumsum_kernel,
        grid=(T // n,),
        in_specs=[spec],
        out_specs=[spec, pl.BlockSpec((LANES, n), lambda i: (0, i))],
        out_shape=[jax.ShapeDtypeStruct((T, LANES), BF16), jax.ShapeDtypeStruct((LANES, T), F32)],
        scratch_shapes=[pltpu.VMEM((1, LANES), F32)],
        compiler_params=_cparams(("arbitrary",)),
        name="fox_cumsum",
    )(f_logit)


FOX_TILE = 512
FOX_KEY_TILE = 256
FOX_STEP_HEADS = 2
FOX_STEP_TILES = 4


FOX_VROWS = FOX_HEAD_DIM + 16


def _fox_kernel(first_ref, k_ref, kaug_ref, qt_ref, ct_ref, vt_ref, o_ref, *scratch):
    t = FOX_TILE
    for u in range(FOX_STEP_TILES):
        cols = slice(u * t, (u + 1) * t)
        _fox_tile(FOX_STEP_TILES * pl.program_id(1) + u, first_ref, k_ref, kaug_ref, qt_ref.at[:, cols],
                  ct_ref.at[:, cols], vt_ref, o_ref.at[cols, :], *(s.at[u] for s in scratch))


def _fox_tile(i, first_ref, k_ref, kaug_ref, qt_ref, ct_ref, vt_ref, o_ref,
              rhs_ref, acc_ref, m_ref, alpha_ref, st_ref, pt_ref):
    p = pl.program_id(0)
    tq, tk = FOX_TILE, FOX_KEY_TILE
    dh = FOX_HEAD_DIM
    H = FOX_HEADS

    NH = FOX_STEP_HEADS
    row = lax.broadcasted_iota(jnp.int32, (LANES, tq), 0)
    for h in range(NH):
        head = NH * p + h
        pair, half = divmod(h, 2)
        qt = qt_ref[pair * LANES:(pair + 1) * LANES, :].astype(F32)
        q_part = jnp.where((row >= half * dh) & (row < (half + 1) * dh), qt, 0.0)
        gate_rows = (row == head) | (row == H + head) | (row == 2 * H + head)
        c_parts = [ct_ref[pl.ds(part * H + head, 1), :] for part in range(3)]
        aug = jnp.where(gate_rows, 1.0, 0.0)
        for part in range(3):
            aug = jnp.where(row == FOX_ONES_ROW + part, c_parts[part], aug)
        rhs_ref[h] = jnp.concatenate([q_part, aug], axis=0).astype(BF16)

    acc_ref[...] = jnp.zeros_like(acc_ref)
    m_ref[...] = jnp.full_like(m_ref, NEG_BIG)
    pt_ref[1] = jnp.zeros(pt_ref.shape[1:], BF16)
    alpha_ref[1] = jnp.ones(alpha_ref.shape[1:], F32)

    def logits(j, slot):
        start = pl.multiple_of(j * tk, tk)
        gates = kaug_ref[pl.ds(start, tk), :]
        for h in range(NH):
            pair = h // 2
            lhs = jnp.concatenate([k_ref[pl.ds(start, tk), pair * LANES:(pair + 1) * LANES], gates], axis=1)
            st_ref[slot, h] = jnp.dot(lhs, rhs_ref[h], preferred_element_type=F32)

    def softmax(slot, mask_offset=None):
        for h in range(NH):
            st = st_ref[slot, h]
            if mask_offset is not None:
                causal = (lax.broadcasted_iota(jnp.int32, (tk, tq), 0) + mask_offset
                          <= lax.broadcasted_iota(jnp.int32, (tk, tq), 1))
                st = jnp.where(causal, st, NEG_BIG)
            m_old = m_ref[h]
            m_new = jnp.maximum(m_old, jnp.max(st, axis=0, keepdims=True))
            alpha_ref[slot, h] = jnp.exp2(m_old - m_new)
            pt_ref[slot, h] = jnp.exp2(st - m_new).astype(BF16)
            m_ref[h] = m_new

    ones_rows = jnp.ones((FOX_VROWS - dh, tk), BF16)

    def accum(j, slot):
        start = pl.multiple_of(j * tk, tk)
        for h in range(NH):
            vt_aug = jnp.concatenate([vt_ref[h * dh:(h + 1) * dh, pl.ds(start, tk)], ones_rows], axis=0)
            pv = jnp.dot(vt_aug, pt_ref[slot, h], preferred_element_type=F32)
            acc_ref[h] = alpha_ref[slot, h] * acc_ref[h] + pv

    sub = tq // tk

    def body(p, carry):
        accum(jnp.maximum(sub * p - 1, 0), 1)
        softmax(0)
        logits(sub * p + 1, 1)
        accum(sub * p, 0)
        softmax(1)
        logits(sub * p + 2, 0)
        return carry

    first = first_ref[p, i]
    logits(sub * first, 0)
    lax.fori_loop(first, i, body, 0)
    logits(sub * i + 1, 1)
    accum(jnp.maximum(sub * i - 1, 0), 1)
    softmax(0, mask_offset=0)
    accum(sub * i, 0)
    softmax(1, mask_offset=tk)
    accum(sub * i + 1, 1)

    for pair in range(NH // 2):
        acc_a, acc_b = acc_ref[2 * pair], acc_ref[2 * pair + 1]
        ot = jnp.concatenate([acc_a[:dh] / acc_a[dh:dh + 1], acc_b[:dh] / acc_b[dh:dh + 1]], axis=0)
        o_ref[:, pair * LANES:(pair + 1) * LANES] = ot.T.astype(o_ref.dtype)


def _fox_norms_kernel(k_ref, qt_ref, kn_ref, qn_ref):
    n = FOX_HEADS * FOX_HEAD_DIM
    chan = lax.broadcasted_iota(jnp.int32, (n, LANES), 0) // FOX_HEAD_DIM
    head = lax.broadcasted_iota(jnp.int32, (n, LANES), 1)
    group = (chan == head).astype(BF16)
    k = k_ref[...].astype(F32)
    k2 = jnp.dot((k * k).astype(BF16), group, preferred_element_type=F32)
    q = qt_ref[...].astype(F32)
    q2 = lax.dot_general((q * q).astype(BF16), group, (((0,), (0,)), ((), ())), preferred_element_type=F32)
    kn_ref[0] = jnp.broadcast_to(jnp.max(k2, axis=0, keepdims=True), kn_ref.shape[1:])
    qn_ref[0] = jnp.broadcast_to(jnp.max(q2, axis=0, keepdims=True), qn_ref.shape[1:])


def fox_norms(k, qt):
    T = k.shape[0]
    t = FOX_TILE
    n = FOX_HEADS * FOX_HEAD_DIM
    out = pl.BlockSpec((1, 8, LANES), lambda i: (i, 0, 0))
    return pl.pallas_call(
        _fox_norms_kernel,
        grid=(T // t,),
        in_specs=[pl.BlockSpec((t, n), lambda i: (i, 0)), pl.BlockSpec((n, t), lambda i: (0, i))],
        out_specs=[out, out],
        out_shape=[jax.ShapeDtypeStruct((T // t, 8, LANES), F32)] * 2,
        compiler_params=_cparams(("parallel",)),
        name="fox_norms",
    )(k, qt)


FOX_UNDERFLOW_LOG2 = 160.0


def fox_first_block(kn2, qn2, ct):
    H = FOX_HEADS
    nt = kn2.shape[0]
    bk = jnp.sqrt(jnp.max(kn2[:, 0, :H], axis=0))
    bq = jnp.sqrt(qn2[:, 0, :H])
    need = 2.0 * 1.02 * bq * bk[None, :] + FOX_UNDERFLOW_LOG2 + 2.0
    c_first = ct[:, 0::FOX_TILE]
    c_last = ct[:, FOX_TILE - 1::FOX_TILE]
    c_first = c_first[:H] + c_first[H:2 * H] + c_first[2 * H:3 * H]
    c_last = c_last[:H] + c_last[H:2 * H] + c_last[2 * H:3 * H]
    gap = c_last[:, None, :] - c_first[:, :, None]
    earlier = jnp.arange(nt)[None, :] < jnp.arange(nt)[:, None]
    prunable = (gap > need.T[:, :, None]) & earlier[None]
    n_skip = jnp.min(jnp.where(prunable, nt, jnp.arange(nt)[None, None, :]), axis=2)
    return jnp.min(n_skip.reshape(H // FOX_STEP_HEADS, FOX_STEP_HEADS, nt), axis=1).astype(jnp.int32)


def fox_attention(first, k, kaug, qt, ct, vt):
    T = k.shape[0]
    t = FOX_TILE
    nt = FOX_STEP_TILES
    nh = FOX_STEP_HEADS
    width = nh * FOX_HEAD_DIM
    return pl.pallas_call(
        _fox_kernel,
        grid=(FOX_HEADS // nh, T // (nt * t)),
        in_specs=[
            pl.BlockSpec(memory_space=pltpu.SMEM),
            pl.BlockSpec((T, width), lambda p, i: (0, p)),
            pl.BlockSpec((T, LANES), lambda p, i: (0, 0), pipeline_mode=pl.Buffered(1)),
            pl.BlockSpec((width, nt * t), lambda p, i: (p, i)),
            pl.BlockSpec((LANES, nt * t), lambda p, i: (0, i)),
            pl.BlockSpec((width, T), lambda p, i: (p, 0)),
        ],
        out_specs=pl.BlockSpec((nt * t, width), lambda p, i: (i, p)),
        out_shape=jax.ShapeDtypeStruct((T, FOX_HEADS * FOX_HEAD_DIM), BF16),
        scratch_shapes=[
            pltpu.VMEM((nt, nh, 2 * LANES, t), BF16),
            pltpu.VMEM((nt, nh, FOX_VROWS, t), F32),
            pltpu.VMEM((nt, nh, 1, t), F32),
            pltpu.VMEM((nt, 2, nh, 1, t), F32),
            pltpu.VMEM((nt, 2, nh, FOX_KEY_TILE, t), F32),
            pltpu.VMEM((nt, 2, nh, FOX_KEY_TILE, t), BF16),
        ],
        compiler_params=_cparams(("parallel", "arbitrary")),
        name="fox",
    )(first, k, kaug, qt, ct, vt)


def kernel(x, norm_mix, norm_mlp, norm_final, w_up, w_down, swa_w_qkv, swa_b_qkv, swa_sinks, swa_w_o,
           hgrn_w_in, hgrn_lb_logits, hgrn_g_norm, hgrn_w_o, fox_w_in, fox_b_in, fox_w_o):
    B, T, D = x.shape
    depth = norm_mix.shape[0]
    lb_soft = jax.nn.softmax(hgrn_lb_logits.astype(F32), axis=0)
    lower_bounds = jnp.cumsum(lb_soft, axis=0) - lb_soft[0]
    w_up_bf16 = w_up.astype(BF16)
    w_down_bf16 = w_down.astype(BF16)

    outs = []
    for b in range(B):
        xb = x[b]
        for i in range(depth):
            m, j = i % N_MIXERS, i // N_MIXERS
            if m == 0:
                nq = SWA_HEADS * SWA_HEAD_DIM
                nkv = SWA_KV_HEADS * SWA_HEAD_DIM
                col_scale = jnp.concatenate([jnp.full((nq,), LOG2E * SWA_HEAD_DIM ** -0.5, F32),
                                             jnp.ones((2 * nkv,), F32)])
                qt, k, vt = norm_proj(xb, norm_mix[i], (swa_w_qkv[j] * col_scale).astype(BF16),
                                      swa_b_qkv[j] * col_scale,
                                      [(0, nq), (nq, nq + nkv), (nq + nkv, nq + 2 * nkv)], [BF16] * 3, "swa_proj",
                                      transposed=(True, False, True))
                a = swa_attention(qt, k, vt, swa_sinks[j])
                w_o = swa_w_o[j]
            elif m == 1:
                hk = HGRN_HEADS * HGRN_EXPAND
                hv = HGRN_HEADS * HGRN_HEAD_DIM
                q, f, v, g = norm_proj(xb, norm_mix[i], hgrn_w_in[j].astype(BF16), jnp.zeros((2 * hk + 2 * hv,), F32),
                                       [(0, hk), (hk, 2 * hk), (2 * hk, 2 * hk + hv), (2 * hk + hv, 2 * hk + 2 * hv)],
                                       [F32, F32, BF16, F32], "hgrn_proj")
                o = hgrn_recurrence(q, f, v, lower_bounds[i])
                a = (o, g, hgrn_g_norm[j])
                w_o = hgrn_w_o[j]
            else:
                n = FOX_HEADS * FOX_HEAD_DIM
                pad = LANES - FOX_HEADS
                col_scale = jnp.concatenate([jnp.full((n,), LOG2E * FOX_HEAD_DIM ** -0.5, F32),
                                             jnp.ones((2 * n + FOX_HEADS,), F32)])
                w = jnp.pad(fox_w_in[j] * col_scale, ((0, 0), (0, pad))).astype(BF16)
                bias = jnp.pad(fox_b_in[j] * col_scale, ((0, pad),))
                qt, k, vt, f = norm_proj(xb, norm_mix[i], w, bias,
                                         [(0, n), (n, 2 * n), (2 * n, 3 * n), (3 * n, 3 * n + LANES)],
                                         [BF16, BF16, BF16, F32], "fox_proj",
                                         transposed=(True, False, True, False))
                kaug, ct = fox_cumsum(f)
                first = fox_first_block(*fox_norms(k, qt), ct)
                a = fox_attention(first, k, kaug, qt, ct, vt)
                w_o = fox_w_o[j]
            last = i == depth - 1
            xb = out_mlp(a, w_o.astype(BF16), xb, norm_mlp[i], w_up_bf16, w_down_bf16, i, norm_final, last)
        outs.append(xb)
    return jnp.stack(outs, axis=0)
```

```python
import functools

import jax
import jax.numpy as jnp
from jax import lax
from jax.experimental import pallas as pl
from jax.experimental.pallas import tpu as pltpu

F32 = jnp.float32
BF16 = jnp.bfloat16

D_MODEL = 1024
D_FF = 4 * D_MODEL
EPS = 1e-6
N_MIXERS = 3

SWA_HEADS = 16
SWA_KV_HEADS = 4
SWA_HEAD_DIM = 64
SWA_WINDOW = 128

HGRN_HEADS = 8
HGRN_EXPAND = 128
HGRN_HEAD_DIM = 128
HGRN_CHUNK = 64
HGRN_SUB = 8
HGRN_STEP_CHUNKS = 2

FOX_HEADS = 16
FOX_HEAD_DIM = 64

LANES = 128
VMEM_LIMIT = 56 * 1024 * 1024
NEG_BIG = -1e30
LOG2E = 1.4426950408889634

ROW_TILE = 1024


def _cparams(sem):
    return pltpu.CompilerParams(dimension_semantics=sem, vmem_limit_bytes=VMEM_LIMIT)


def _const_spec(shape):
    nd = len(shape)
    return pl.BlockSpec(shape, lambda *_: (0,) * nd, pipeline_mode=pl.Buffered(1))


def _layer_spec(shape, layer):
    nd = len(shape)
    return pl.BlockSpec((None,) + tuple(shape), lambda *_: (layer,) + (0,) * nd, pipeline_mode=pl.Buffered(1))


def _rms(x, g):
    ms = jnp.mean(x * x, axis=-1, keepdims=True)
    return x * lax.rsqrt(ms + EPS) * g


def _norm_proj_kernel(x_ref, g_ref, w_ref, b_ref, *o_refs, segments, transposed, n_chunk):
    h = _rms(x_ref[...], g_ref[...]).astype(BF16)
    for o_ref, (lo, hi), tr in zip(o_refs, segments, transposed):
        for j in range(lo, hi, n_chunk):
            je = min(j + n_chunk, hi)
            y = jnp.dot(h, w_ref[:, j:je], preferred_element_type=F32) + b_ref[:, j:je]
            if tr:
                o_ref[j - lo:je - lo, :] = y.T.astype(o_ref.dtype)
            else:
                o_ref[:, j - lo:je - lo] = y.astype(o_ref.dtype)


def norm_proj(x, g, w, b, segments, dtypes, name, transposed=None):
    T, D = x.shape
    N = w.shape[1]
    tm = ROW_TILE
    transposed = tuple(transposed) if transposed is not None else (False,) * len(segments)
    out_shape, out_specs = [], []
    for (lo, hi), dt, tr in zip(segments, dtypes, transposed):
        if tr:
            out_shape.append(jax.ShapeDtypeStruct((hi - lo, T), dt))
            out_specs.append(pl.BlockSpec((hi - lo, tm), lambda i: (0, i)))
        else:
            out_shape.append(jax.ShapeDtypeStruct((T, hi - lo), dt))
            out_specs.append(pl.BlockSpec((tm, hi - lo), lambda i: (i, 0)))
    return pl.pallas_call(
        functools.partial(_norm_proj_kernel, segments=tuple(segments), transposed=transposed, n_chunk=512),
        grid=(T // tm,),
        in_specs=[
            pl.BlockSpec((tm, D), lambda i: (i, 0)),
            _const_spec((1, D)),
            _const_spec((D, N)),
            _const_spec((1, N)),
        ],
        out_specs=out_specs,
        out_shape=out_shape,
        compiler_params=_cparams(("parallel",)),
        name=name,
    )(x, g.reshape(1, D), w, b.reshape(1, N))


def _hgrn_gate(r_ref, gate_ref, gain_ref):
    parts = []
    for h in range(HGRN_HEADS):
        sl = slice(h * HGRN_HEAD_DIM, (h + 1) * HGRN_HEAD_DIM)
        r = r_ref[:, sl]
        r = r * lax.rsqrt(jnp.mean(r * r, axis=-1, keepdims=True) + EPS)
        g = gate_ref[:, sl]
        parts.append((r * gain_ref[:, sl] * (g * jax.nn.sigmoid(g))).astype(BF16))
    return jnp.concatenate(parts, axis=1)


def _mlp_tail(x, g_ref, wu_ref, wd_ref, gf_ref, o_ref, ff_chunk, final_norm):
    h = _rms(x, g_ref[...]).astype(BF16)
    acc = x
    for j in range(0, D_FF, ff_chunk):
        u = jnp.dot(h, wu_ref[:, j:j + ff_chunk], preferred_element_type=F32)
        u = jnp.maximum(u, 0.0)
        u = (u * u).astype(BF16)
        acc = acc + jnp.dot(u, wd_ref[j:j + ff_chunk, :], preferred_element_type=F32)
    if final_norm:
        acc = _rms(acc, gf_ref[...])
    o_ref[...] = acc


def _out_mlp_kernel(a_ref, wo_ref, x_ref, g_ref, wu_ref, wd_ref, gf_ref, o_ref, *, ff_chunk, final_norm):
    x = x_ref[...] + jnp.dot(a_ref[...], wo_ref[...], preferred_element_type=F32)
    _mlp_tail(x, g_ref, wu_ref, wd_ref, gf_ref, o_ref, ff_chunk, final_norm)


def _gated_out_mlp_kernel(r_ref, gate_ref, gain_ref, wo_ref, x_ref, g_ref, wu_ref, wd_ref, gf_ref, o_ref, *,
                          ff_chunk, final_norm):
    a = _hgrn_gate(r_ref, gate_ref, gain_ref)
    x = x_ref[...] + jnp.dot(a, wo_ref[...], preferred_element_type=F32)
    _mlp_tail(x, g_ref, wu_ref, wd_ref, gf_ref, o_ref, ff_chunk, final_norm)


def out_mlp(mixer_out, w_o, x, g, w_up_all, w_down_all, layer, g_final, final_norm):
    T, D = x.shape
    gated = isinstance(mixer_out, tuple)
    tm = ROW_TILE // 2 if gated else ROW_TILE
    row = pl.BlockSpec((tm, D), lambda i: (i, 0))
    if gated:
        r, gate, gain = mixer_out
        body, mix_args, mix_specs = _gated_out_mlp_kernel, (r, gate, gain.reshape(1, D)), [row, row, _const_spec((1, D))]
    else:
        body, mix_args, mix_specs = _out_mlp_kernel, (mixer_out,), [row]
    return pl.pallas_call(
        functools.partial(body, ff_chunk=512, final_norm=final_norm),
        grid=(T // tm,),
        in_specs=mix_specs + [
            _const_spec((D, D)),
            row,
            _const_spec((1, D)),
            _layer_spec((D, D_FF), layer),
            _layer_spec((D_FF, D), layer),
            _const_spec((1, D)),
        ],
        out_specs=row,
        out_shape=jax.ShapeDtypeStruct((T, D), F32),
        compiler_params=_cparams(("parallel",)),
        name="out_mlp",
    )(*mix_args, w_o, x, g.reshape(1, D), w_up_all, w_down_all, g_final.reshape(1, D))


SWA_VROWS = SWA_HEAD_DIM + 16
SWA_STEP_BLOCKS = 4


def _swa_kernel(sink_ref, qt_ref, kc_ref, kp_ref, vtc_ref, vtp_ref, o_ref):
    n = pl.program_id(0)
    W, dh, NB = SWA_WINDOW, SWA_HEAD_DIM, SWA_STEP_BLOCKS
    G = SWA_HEADS // SWA_KV_HEADS
    key = lax.broadcasted_iota(jnp.int32, (2 * W, G * W), 0)
    lane = lax.broadcasted_iota(jnp.int32, (2 * W, G * W), 1)
    rel = lane % W + W - key
    in_window = (rel >= 0) & (rel < W)
    group_lane = lax.broadcasted_iota(jnp.int32, (1, G * W), 1) // W
    k_all = jnp.concatenate([kp_ref[...], kc_ref[...]], axis=0)
    vt_all = jnp.concatenate([vtp_ref[...], vtc_ref[...]], axis=1)
    ones_rows = jnp.ones((SWA_VROWS - dh, 2 * W), BF16)
    units = [(b, hk) for b in range(NB) for hk in range(SWA_KV_HEADS)]

    sts = []
    for b, hk in units:
        pair, half = divmod(hk, 2)
        lhs = k_all[b * W:(b + 2) * W, pair * LANES:(pair + 1) * LANES]
        q_cat = jnp.concatenate([qt_ref[(hk * G + g) * dh:(hk * G + g + 1) * dh, b * W:(b + 1) * W]
                                 for g in range(G)], axis=1)
        zeros = jnp.zeros_like(q_cat)
        rhs = jnp.concatenate([q_cat, zeros] if half == 0 else [zeros, q_cat], axis=0)
        sts.append(jnp.dot(lhs, rhs, preferred_element_type=F32))

    pts, sink_terms = [], []
    for (b, hk), st in zip(units, sts):
        visible = in_window if b > 0 else in_window & ((n > 0) | (key >= W))
        st = jnp.where(visible, st, NEG_BIG)
        sink = jnp.zeros((1, G * W), F32)
        for g in range(G):
            sink = jnp.where(group_lane == g, sink_ref[hk * G + g] * LOG2E, sink)
        m = jnp.maximum(jnp.max(st, axis=0, keepdims=True), sink)
        pts.append(jnp.exp2(st - m).astype(BF16))
        sink_terms.append(jnp.exp2(sink - m))

    accs = []
    for (b, hk), pt in zip(units, pts):
        vt_aug = jnp.concatenate([vt_all[hk * dh:(hk + 1) * dh, b * W:(b + 2) * W], ones_rows], axis=0)
        accs.append(jnp.dot(vt_aug, pt, preferred_element_type=F32))

    outs = [[] for _ in range(NB)]
    for (b, hk), acc, sink_term in zip(units, accs, sink_terms):
        ot = acc[:dh] / (acc[dh:dh + 1] + sink_term)
        for g in range(0, G, 2):
            two = jnp.concatenate([ot[:, g * W:(g + 1) * W], ot[:, (g + 1) * W:(g + 2) * W]], axis=0)
            outs[b].append(two.T)
    for b in range(NB):
        o_ref[b * W:(b + 1) * W, :] = jnp.concatenate(outs[b], axis=1).astype(o_ref.dtype)


def swa_attention(qt, k, vt, sinks):
    T = k.shape[0]
    W, NB = SWA_WINDOW, SWA_STEP_BLOCKS
    nq = SWA_HEADS * SWA_HEAD_DIM
    kvw = SWA_KV_HEADS * SWA_HEAD_DIM
    prev = lambda n: jnp.maximum(NB * n - 1, 0)
    return pl.pallas_call(
        _swa_kernel,
        grid=(T // (NB * W),),
        in_specs=[
            pl.BlockSpec(memory_space=pltpu.SMEM),
            pl.BlockSpec((nq, NB * W), lambda n: (0, n)),
            pl.BlockSpec((NB * W, kvw), lambda n: (n, 0)),
            pl.BlockSpec((W, kvw), lambda n: (prev(n), 0)),
            pl.BlockSpec((kvw, NB * W), lambda n: (0, n)),
            pl.BlockSpec((kvw, W), lambda n: (0, prev(n))),
        ],
        out_specs=pl.BlockSpec((NB * W, nq), lambda n: (n, 0)),
        out_shape=jax.ShapeDtypeStruct((T, nq), BF16),
        compiler_params=_cparams(("parallel",)),
        name="swa",
    )(sinks, qt, k, k, vt, vt)


def _split2(x):
    hi = x.astype(BF16)
    lo = (x - hi.astype(F32)).astype(BF16)
    return hi, lo


def _split_dot(m_bf16, parts):
    hi, lo = parts
    return (jnp.dot(m_bf16, hi, preferred_element_type=F32)
            + jnp.dot(m_bf16, lo, preferred_element_type=F32))


def _hgrn_kernel(q_ref, f_ref, v_ref, lb_ref, o_ref, st_ref):
    C, SUB, K, H = HGRN_CHUNK, HGRN_SUB, HGRN_EXPAND, HGRN_HEADS
    R = C * HGRN_STEP_CHUNKS
    nt = (((1,), (1,)), ((), ()))
    tn = (((0,), (0,)), ((), ()))

    @pl.when(pl.program_id(0) == 0)
    def _():
        st_ref[...] = jnp.zeros_like(st_ref)

    row = lax.broadcasted_iota(jnp.int32, (R, R), 0)
    col = lax.broadcasted_iota(jnp.int32, (R, R), 1)
    same_chunk = row // C == col // C
    heads = [slice(h * K, (h + 1) * K) for h in range(H)]
    chunks = [slice(c * C, (c + 1) * C) for c in range(HGRN_STEP_CHUNKS)]

    lb = lb_ref[...]
    f = lb + (1.0 - lb) * jax.nn.sigmoid(f_ref[...])
    logf = _split2(jnp.log2(f))
    kk = 1.0 - f
    q = q_ref[...]
    qs = q * jax.nn.sigmoid(q)
    v = v_ref[...]
    bcum = _split_dot(((row >= col) & same_chunk).astype(BF16), logf)
    to_end = _split_dot(((row < col) & same_chunk).astype(BF16), logf)

    q_in = (qs * jnp.exp2(bcum)).astype(BF16)
    k_out = (kk * jnp.exp2(to_end)).astype(BF16)
    o_in = [[] for _ in heads]
    for cs in chunks:
        decay = jnp.exp2(bcum[cs.stop - 1:cs.stop, :])
        for h, hs in enumerate(heads):
            st = st_ref[h]
            o_in[h].append(lax.dot_general(q_in[cs, hs], st.astype(BF16), nt, preferred_element_type=F32))
            upd = lax.dot_general(v[cs, hs], k_out[cs, hs], tn, preferred_element_type=F32)
            st_ref[h] = st * decay[:, hs] + upd
    o = [jnp.concatenate(parts, axis=0) for parts in o_in]


    a = [jnp.zeros((R, R), F32) for _ in heads]
    half = SUB
    while half < C:
        pos = row % (2 * half)
        bnd = row - pos + half - 1
        to_bnd = (((pos >= half) & (col > bnd) & (col <= row))
                  | ((pos < half) & (col > row) & (col <= bnd)))
        e = jnp.exp2(_split_dot(to_bnd.astype(BF16), logf))
        q_e = (qs * e).astype(BF16)
        k_e = (kk * e).astype(BF16)
        right_left = (pos >= half) & (col <= bnd) & (col > bnd - half)
        for h, hs in enumerate(heads):
            a_l = lax.dot_general(q_e[:, hs], k_e[:, hs], nt, preferred_element_type=F32)
            a[h] = jnp.where(right_left, a_l, a[h])
        half *= 2

    sub_pos = lax.broadcasted_iota(jnp.int32, (R, H * K), 0) % SUB
    for d in range(SUB):
        if d == 0:
            e = qs * kk
        else:
            e = qs * pltpu.roll(kk, d, 0) * jnp.exp2(bcum - pltpu.roll(bcum, d, 0))
            e = jnp.where(sub_pos >= d, e, 0.0)
        band = (row - col == d) & (row % SUB >= d)
        for h, hs in enumerate(heads):
            a[h] = jnp.where(band, jnp.sum(e[:, hs], axis=1, keepdims=True), a[h])

    for h, hs in enumerate(heads):
        o[h] = o[h] + jnp.dot(a[h].astype(BF16), v[:, hs], preferred_element_type=F32)
    o_ref[...] = jnp.concatenate(o, axis=1)


def hgrn_recurrence(q, f_logit, v, lb):
    T = q.shape[0]
    rows = HGRN_CHUNK * HGRN_STEP_CHUNKS
    width = HGRN_HEADS * HGRN_EXPAND
    blk = pl.BlockSpec((rows, width), lambda c: (c, 0))
    return pl.pallas_call(
        _hgrn_kernel,
        grid=(T // rows,),
        in_specs=[blk, blk, blk, _const_spec((1, width))],
        out_specs=blk,
        out_shape=jax.ShapeDtypeStruct((T, width), F32),
        scratch_shapes=[pltpu.VMEM((HGRN_HEADS, HGRN_HEAD_DIM, HGRN_EXPAND), F32)],
        compiler_params=_cparams(("arbitrary",)),
        name="hgrn",
    )(q, f_logit, v, lb.reshape(1, -1))


FOX_CUM_TILE = 512


def _split3(x):
    hi = x.astype(BF16)
    r1 = x - hi.astype(F32)
    mid = r1.astype(BF16)
    lo = (r1 - mid.astype(F32)).astype(BF16)
    return hi, mid, lo


FOX_ONES_ROW = 3 * FOX_HEADS


def _fox_cumsum_kernel(f_ref, kaug_ref, ct_ref, carry_ref):
    @pl.when(pl.program_id(0) == 0)
    def _():
        carry_ref[...] = jnp.zeros_like(carry_ref)

    x = f_ref[...]
    ls = jnp.minimum(x, 0.0) - jnp.log(1.0 + jnp.exp(-jnp.abs(x)))
    n = FOX_CUM_TILE
    tril = (lax.broadcasted_iota(jnp.int32, (n, n), 0) >= lax.broadcasted_iota(jnp.int32, (n, n), 1)).astype(BF16)
    hi, mid, lo = _split3(ls)
    cum = (jnp.dot(tril, hi, preferred_element_type=F32)
           + jnp.dot(tril, mid, preferred_element_type=F32)
           + jnp.dot(tril, lo, preferred_element_type=F32))
    c = cum + carry_ref[...]
    carry_ref[...] = c[n - 1:n, :]
    H = FOX_HEADS
    hi2, mid2, lo2 = (a.astype(F32) for a in _split3(c * LOG2E))
    lane = lax.broadcasted_iota(jnp.int32, (n, LANES), 1)
    packed = jnp.where(lane < H, hi2,
                       jnp.where(lane < 2 * H, pltpu.roll(mid2, H, 1),
                                 jnp.where(lane < 3 * H, pltpu.roll(lo2, 2 * H, 1), 0.0)))
    is_one = (lane >= FOX_ONES_ROW) & (lane < FOX_ONES_ROW + 3)
    kaug_ref[...] = jnp.where(is_one, 1.0, -packed).astype(BF16)
    ct_ref[...] = packed.T


FOX_TILE = 512
FOX_KEY_TILE = 256
FOX_STEP_HEADS = 2
FOX_STEP_TILES = 4


FOX_VROWS = FOX_HEAD_DIM + 16


def _fox_kernel(first_ref, k_ref, kaug_ref, qt_ref, ct_ref, vt_ref, o_ref, *scratch):
    t = FOX_TILE
    for u in range(FOX_STEP_TILES):
        cols = slice(u * t, (u + 1) * t)
        _fox_tile(FOX_STEP_TILES * pl.program_id(1) + u, first_ref, k_ref, kaug_ref, qt_ref.at[:, cols],
                  ct_ref.at[:, cols], vt_ref, o_ref.at[cols, :], *(s.at[u] for s in scratch))


def _fox_tile(i, first_ref, k_ref, kaug_ref, qt_ref, ct_ref, vt_ref, o_ref,
              rhs_ref, acc_ref, m_ref, alpha_ref, st_ref, pt_ref):
    p = pl.program_id(0)
    tq, tk = FOX_TILE, FOX_KEY_TILE
    dh = FOX_HEAD_DIM
    H = FOX_HEADS

    NH = FOX_STEP_HEADS
    row = lax.broadcasted_iota(jnp.int32, (LANES, tq), 0)
    for h in range(NH):
        head = NH * p + h
        pair, half = divmod(h, 2)
        qt = qt_ref[pair * LANES:(pair + 1) * LANES, :].astype(F32)
        q_part = jnp.where((row >= half * dh) & (row < (half + 1) * dh), qt, 0.0)
        gate_rows = (row == head) | (row == H + head) | (row == 2 * H + head)
        c_parts = [ct_ref[pl.ds(part * H + head, 1), :] for part in range(3)]
        aug = jnp.where(gate_rows, 1.0, 0.0)
        for part in range(3):
            aug = jnp.where(row == FOX_ONES_ROW + part, c_parts[part], aug)
        rhs_ref[h] = jnp.concatenate([q_part, aug], axis=0).astype(BF16)

    acc_ref[...] = jnp.zeros_like(acc_ref)
    m_ref[...] = jnp.full_like(m_ref, NEG_BIG)
    pt_ref[1] = jnp.zeros(pt_ref.shape[1:], BF16)
    alpha_ref[1] = jnp.ones(alpha_ref.shape[1:], F32)

    def logits(j, slot):
        start = pl.multiple_of(j * tk, tk)
        gates = kaug_ref[pl.ds(start, tk), :]
        for h in range(NH):
            pair = h // 2
            lhs = jnp.concatenate([k_ref[pl.ds(start, tk), pair * LANES:(pair + 1) * LANES], gates], axis=1)
            st_ref[slot, h] = jnp.dot(lhs, rhs_ref[h], preferred_element_type=F32)

    def softmax(slot, mask_offset=None):
        for h in range(NH):
            st = st_ref[slot, h]
            if mask_offset is not None:
                causal = (lax.broadcasted_iota(jnp.int32, (tk, tq), 0) + mask_offset
                          <= lax.broadcasted_iota(jnp.int32, (tk, tq), 1))
                st = jnp.where(causal, st, NEG_BIG)
            m_old = m_ref[h]
            m_new = jnp.maximum(m_old, jnp.max(st, axis=0, keepdims=True))
            alpha_ref[slot, h] = jnp.exp2(m_old - m_new)
            pt_ref[slot, h] = jnp.exp2(st - m_new).astype(BF16)
            m_ref[h] = m_new

    ones_rows = jnp.ones((FOX_VROWS - dh, tk), BF16)

    def accum(j, slot):
        start = pl.multiple_of(j * tk, tk)
        for h in range(NH):
            vt_aug = jnp.concatenate([vt_ref[h * dh:(h + 1) * dh, pl.ds(start, tk)], ones_rows], axis=0)
            pv = jnp.dot(vt_aug, pt_ref[slot, h], preferred_element_type=F32)
            acc_ref[h] = alpha_ref[slot, h] * acc_ref[h] + pv

    sub = tq // tk

    def body(p, carry):
        accum(jnp.maximum(sub * p - 1, 0), 1)
        softmax(0)
        logits(sub * p + 1, 1)
        accum(sub * p, 0)
        softmax(1)
        logits(sub * p + 2, 0)
        return carry

    first = first_ref[p, i]
    logits(sub * first, 0)
    lax.fori_loop(first, i, body, 0)
    logits(sub * i + 1, 1)
    accum(jnp.maximum(sub * i - 1, 0), 1)
    softmax(0, mask_offset=0)
    accum(sub * i, 0)
    softmax(1, mask_offset=tk)
    accum(sub * i + 1, 1)

    for pair in range(NH // 2):
        acc_a, acc_b = acc_ref[2 * pair], acc_ref[2 * pair + 1]
        ot = jnp.concatenate([acc_a[:dh] / acc_a[dh:dh + 1], acc_b[:dh] / acc_b[dh:dh + 1]], axis=0)
        o_ref[:, pair * LANES:(pair + 1) * LANES] = ot.T.astype(o_ref.dtype)


def _fox_norms_kernel(k_ref, qt_ref, kn_ref, qn_ref):
    n = FOX_HEADS * FOX_HEAD_DIM
    chan = lax.broadcasted_iota(jnp.int32, (n, LANES), 0) // FOX_HEAD_DIM
    head = lax.broadcasted_iota(jnp.int32, (n, LANES), 1)
    group = (chan == head).astype(BF16)
    k = k_ref[...].astype(F32)
    k2 = jnp.dot((k * k).astype(BF16), group, preferred_element_type=F32)
    q = qt_ref[...].astype(F32)
    q2 = lax.dot_general((q * q).astype(BF16), group, (((0,), (0,)), ((), ())), preferred_element_type=F32)
    kn_ref[0] = jnp.broadcast_to(jnp.max(k2, axis=0, keepdims=True), kn_ref.shape[1:])
    qn_ref[0] = jnp.broadcast_to(jnp.max(q2, axis=0, keepdims=True), qn_ref.shape[1:])


def _fox_stats_kernel(f_ref, k_ref, qt_ref, kaug_ref, ct_ref, kn_ref, qn_ref, carry_ref):
    _fox_cumsum_kernel(f_ref, kaug_ref, ct_ref, carry_ref)
    _fox_norms_kernel(k_ref, qt_ref, kn_ref, qn_ref)


def fox_stats(f_logit, k, qt):
    assert FOX_CUM_TILE == FOX_TILE
    T = k.shape[0]
    t = FOX_TILE
    n = FOX_HEADS * FOX_HEAD_DIM
    rows = pl.BlockSpec((t, LANES), lambda i: (i, 0))
    norm = pl.BlockSpec((1, 8, LANES), lambda i: (i, 0, 0))
    return pl.pallas_call(
        _fox_stats_kernel,
        grid=(T // t,),
        in_specs=[rows, pl.BlockSpec((t, n), lambda i: (i, 0)), pl.BlockSpec((n, t), lambda i: (0, i))],
        out_specs=[rows, pl.BlockSpec((LANES, t), lambda i: (0, i)), norm, norm],
        out_shape=[jax.ShapeDtypeStruct((T, LANES), BF16), jax.ShapeDtypeStruct((LANES, T), F32),
                   jax.ShapeDtypeStruct((T // t, 8, LANES), F32), jax.ShapeDtypeStruct((T // t, 8, LANES), F32)],
        scratch_shapes=[pltpu.VMEM((1, LANES), F32)],
        compiler_params=_cparams(("arbitrary",)),
        name="fox_stats",
    )(f_logit, k, qt)


FOX_UNDERFLOW_LOG2 = 160.0


def fox_first_block(kn2, qn2, ct):
    H = FOX_HEADS
    nt = kn2.shape[0]
    bk = jnp.sqrt(jnp.max(kn2[:, 0, :H], axis=0))
    bq = jnp.sqrt(qn2[:, 0, :H])
    need = 2.0 * 1.02 * bq * bk[None, :] + FOX_UNDERFLOW_LOG2 + 2.0
    c_first = ct[:, 0::FOX_TILE]
    c_last = ct[:, FOX_TILE - 1::FOX_TILE]
    c_first = c_first[:H] + c_first[H:2 * H] + c_first[2 * H:3 * H]
    c_last = c_last[:H] + c_last[H:2 * H] + c_last[2 * H:3 * H]
    gap = c_last[:, None, :] - c_first[:, :, None]
    earlier = jnp.arange(nt)[None, :] < jnp.arange(nt)[:, None]
    prunable = (gap > need.T[:, :, None]) & earlier[None]
    n_skip = jnp.min(jnp.where(prunable, nt, jnp.arange(nt)[None, None, :]), axis=2)
    return jnp.min(n_skip.reshape(H // FOX_STEP_HEADS, FOX_STEP_HEADS, nt), axis=1).astype(jnp.int32)


def fox_attention(first, k, kaug, qt, ct, vt):
    T = k.shape[0]
    t = FOX_TILE
    nt = FOX_STEP_TILES
    nh = FOX_STEP_HEADS
    width = nh * FOX_HEAD_DIM
    return pl.pallas_call(
        _fox_kernel,
        grid=(FOX_HEADS // nh, T // (nt * t)),
        in_specs=[
            pl.BlockSpec(memory_space=pltpu.SMEM),
            pl.BlockSpec((T, width), lambda p, i: (0, p)),
            pl.BlockSpec((T, LANES), lambda p, i: (0, 0), pipeline_mode=pl.Buffered(1)),
            pl.BlockSpec((width, nt * t), lambda p, i: (p, i)),
            pl.BlockSpec((LANES, nt * t), lambda p, i: (0, i)),
            pl.BlockSpec((width, T), lambda p, i: (p, 0)),
        ],
        out_specs=pl.BlockSpec((nt * t, width), lambda p, i: (i, p)),
        out_shape=jax.ShapeDtypeStruct((T, FOX_HEADS * FOX_HEAD_DIM), BF16),
        scratch_shapes=[
            pltpu.VMEM((nt, nh, 2 * LANES, t), BF16),
            pltpu.VMEM((nt, nh, FOX_VROWS, t), F32),
            pltpu.VMEM((nt, nh, 1, t), F32),
            pltpu.VMEM((nt, 2, nh, 1, t), F32),
            pltpu.VMEM((nt, 2, nh, FOX_KEY_TILE, t), F32),
            pltpu.VMEM((nt, 2, nh, FOX_KEY_TILE, t), BF16),
        ],
        compiler_params=_cparams(("parallel", "arbitrary")),
        name="fox",
    )(first, k, kaug, qt, ct, vt)


def kernel(x, norm_mix, norm_mlp, norm_final, w_up, w_down, swa_w_qkv, swa_b_qkv, swa_sinks, swa_w_o,
           hgrn_w_in, hgrn_lb_logits, hgrn_g_norm, hgrn_w_o, fox_w_in, fox_b_in, fox_w_o):
    B, T, D = x.shape
    depth = norm_mix.shape[0]
    lb_soft = jax.nn.softmax(hgrn_lb_logits.astype(F32), axis=0)
    lower_bounds = jnp.cumsum(lb_soft, axis=0) - lb_soft[0]
    w_up_bf16 = w_up.astype(BF16)
    w_down_bf16 = w_down.astype(BF16)

    outs = []
    for b in range(B):
        xb = x[b]
        for i in range(depth):
            m, j = i % N_MIXERS, i // N_MIXERS
            if m == 0:
                nq = SWA_HEADS * SWA_HEAD_DIM
                nkv = SWA_KV_HEADS * SWA_HEAD_DIM
                col_scale = jnp.concatenate([jnp.full((nq,), LOG2E * SWA_HEAD_DIM ** -0.5, F32),
                                             jnp.ones((2 * nkv,), F32)])
                qt, k, vt = norm_proj(xb, norm_mix[i], (swa_w_qkv[j] * col_scale).astype(BF16),
                                      swa_b_qkv[j] * col_scale,
                                      [(0, nq), (nq, nq + nkv), (nq + nkv, nq + 2 * nkv)], [BF16] * 3, "swa_proj",
                                      transposed=(True, False, True))
                a = swa_attention(qt, k, vt, swa_sinks[j])
                w_o = swa_w_o[j]
            elif m == 1:
                hk = HGRN_HEADS * HGRN_EXPAND
                hv = HGRN_HEADS * HGRN_HEAD_DIM
                q, f, v, g = norm_proj(xb, norm_mix[i], hgrn_w_in[j].astype(BF16), jnp.zeros((2 * hk + 2 * hv,), F32),
                                       [(0, hk), (hk, 2 * hk), (2 * hk, 2 * hk + hv), (2 * hk + hv, 2 * hk + 2 * hv)],
                                       [F32, F32, BF16, F32], "hgrn_proj")
                o = hgrn_recurrence(q, f, v, lower_bounds[i])
                a = (o, g, hgrn_g_norm[j])
                w_o = hgrn_w_o[j]
            else:
                n = FOX_HEADS * FOX_HEAD_DIM
                pad = LANES - FOX_HEADS
                col_scale = jnp.concatenate([jnp.full((n,), LOG2E * FOX_HEAD_DIM ** -0.5, F32),
                                             jnp.ones((2 * n + FOX_HEADS,), F32)])
                w = jnp.pad(fox_w_in[j] * col_scale, ((0, 0), (0, pad))).astype(BF16)
                bias = jnp.pad(fox_b_in[j] * col_scale, ((0, pad),))
                qt, k, vt, f = norm_proj(xb, norm_mix[i], w, bias,
                                         [(0, n), (n, 2 * n), (2 * n, 3 * n), (3 * n, 3 * n + LANES)],
                                         [BF16, BF16, BF16, F32], "fox_proj",
                                         transposed=(True, False, True, False))
                kaug, ct, kn2, qn2 = fox_stats(f, k, qt)
                first = fox_first_block(kn2, qn2, ct)
                a = fox_attention(first, k, kaug, qt, ct, vt)
                w_o = fox_w_o[j]
            last = i == depth - 1
            xb = out_mlp(a, w_o.astype(BF16), xb, norm_mlp[i], w_up_bf16, w_down_bf16, i, norm_final, last)
        outs.append(xb)
    return jnp.stack(outs, axis=0)
```
